```python
import jax, jax.numpy as jnp
from jax import lax
import numpy as np

D_MODEL = 1024
BATCH = 4
SEQ = 8192
DEPTH = 2

CHUNK = 64
N_MIXERS = 2
POOL_WINDOWS = (2, 4, 8, 16)
N_POOL_GROUPS = len(POOL_WINDOWS)
POOL_GROUP_DIM = D_MODEL // N_POOL_GROUPS
HEAD_DIM = 64
N_HEADS = D_MODEL // HEAD_DIM
Q_BLOCK = 128
D_FF_DENSE = ((8 * D_MODEL // 3 + 127) // 128) * 128
N_EXPERTS = 8
TOP_K = 2
D_FF_EXPERT = 7 * D_MODEL // 2
RMS_EPS = 1e-6
N_POOL_LAYERS = (DEPTH + 1) // 2
N_ATTN_LAYERS = DEPTH // 2
N_DENSE_LAYERS = (DEPTH + 1) // 2
N_MOE_LAYERS = DEPTH // 2
NEG_INF = -1e30

kernel_name = "hybrid_pool_fox_moe_trunk"


def rms_norm(x, g):
    xf = x.astype(jnp.float32)
    y = xf * lax.rsqrt(jnp.mean(xf * xf, axis=-1, keepdims=True) + RMS_EPS)
    return (y * g.astype(jnp.float32)).astype(x.dtype)


def pool_mixer(h, w_group, scale):
    B, S, D = h.shape
    hf = h.astype(jnp.float32)
    cum = jnp.cumsum(hf, axis=1)
    t = jnp.arange(S)
    outs = []
    for g, w in enumerate(POOL_WINDOWS):
        sl = slice(g * POOL_GROUP_DIM, (g + 1) * POOL_GROUP_DIM)
        c = cum[:, :, sl]
        c_lag = jnp.pad(c, ((0, 0), (w, 0), (0, 0)))[:, :S]
        count = jnp.minimum(t + 1, w).astype(jnp.float32)
        mean = (c - c_lag) / count[None, :, None]
        outs.append(mean - hf[:, :, sl])
    pooled = jnp.stack(outs, axis=2).astype(h.dtype)
    mixed = jnp.einsum('bsgc,gcd->bsgd', pooled, w_group).reshape(B, S, D)
    return mixed * scale


def forgetting_attention(h, w_in, b_f, w_out):
    B, S, D = h.shape
    proj = h @ w_in
    q, k, v, f_logit = jnp.split(proj, [D, 2 * D, 3 * D], axis=-1)
    q = q.reshape(B, S, N_HEADS, HEAD_DIM)
    k = k.reshape(B, S, N_HEADS, HEAD_DIM)
    v = v.reshape(B, S, N_HEADS, HEAD_DIM)
    log_f = jax.nn.log_sigmoid((f_logit + b_f).astype(jnp.float32))
    c = jnp.cumsum(log_f, axis=1).transpose(0, 2, 1)
    n_blk = S // Q_BLOCK
    q_blocks = q.reshape(B, n_blk, Q_BLOCK, N_HEADS, HEAD_DIM).transpose(1, 0, 3, 2, 4)
    c_blocks = c.reshape(B, N_HEADS, n_blk, Q_BLOCK).transpose(2, 0, 1, 3)
    pos_k = jnp.arange(S)
    scale = HEAD_DIM ** -0.5

    def one_block(args):
        q_i, c_i, i = args
        pos_q = i * Q_BLOCK + jnp.arange(Q_BLOCK)
        logits = jnp.einsum('bhqd,bshd->bhqs', q_i, k).astype(jnp.float32) * scale
        logits = logits + c_i[..., :, None] - c[:, :, None, :]
        mask = pos_q[:, None] >= pos_k[None, :]
        logits = jnp.where(mask[None, None], logits, NEG_INF)
        p = jax.nn.softmax(logits, axis=-1).astype(v.dtype)
        return jnp.einsum('bhqs,bshd->bqhd', p, v)

    o = lax.map(one_block, (q_blocks, c_blocks, jnp.arange(n_blk)))
    o = o.transpose(1, 0, 2, 3, 4).reshape(B, S, D)
    return o @ w_out


def swiglu(h, w_gate, w_up, w_down):
    return (jax.nn.silu(h @ w_gate) * (h @ w_up)) @ w_down


def moe_swiglu(h, w_router, w_gate, w_up, w_down):
    logits = (h @ w_router).astype(jnp.float32)
    top_val, top_idx = lax.top_k(logits, TOP_K)
    top_w = jax.nn.softmax(top_val, axis=-1)
    gates = jnp.sum(jax.nn.one_hot(top_idx, N_EXPERTS, dtype=jnp.float32) * top_w[..., None], axis=-2)
    gates = gates.astype(h.dtype)
    out = jnp.zeros_like(h)
    for e in range(N_EXPERTS):
        out = out + gates[..., e:e + 1] * swiglu(h, w_gate[e], w_up[e], w_down[e])
    return out


def setup_inputs(seed: int = 0) -> dict:
    key = jax.random.key(seed)
    ks = jax.random.split(key, 16)
    D, H = D_MODEL, N_HEADS

    def nrm(k, shape, fan_in):
        return jax.random.normal(k, shape, jnp.float32) * (fan_in ** -0.5)

    x = jax.random.normal(ks[0], (BATCH, SEQ, D), jnp.float32)
    norm_g = 1.0 + 0.05 * jax.random.normal(ks[1], (DEPTH, 4, D), jnp.float32)
    pool_w = nrm(ks[2], (N_POOL_LAYERS, N_POOL_GROUPS, POOL_GROUP_DIM, POOL_GROUP_DIM), POOL_GROUP_DIM)
    pool_scale = 1.0 + 0.1 * jax.random.normal(ks[3], (N_POOL_LAYERS, D), jnp.float32)
    attn_w_in = nrm(ks[4], (N_ATTN_LAYERS, D, 3 * D + H), D)
    attn_b_f = 3.0 + 0.5 * jax.random.normal(ks[5], (N_ATTN_LAYERS, H), jnp.float32)
    attn_w_out = nrm(ks[6], (N_ATTN_LAYERS, D, D), D)
    ffn_w_gate = nrm(ks[7], (N_DENSE_LAYERS, D, D_FF_DENSE), D)
    ffn_w_up = nrm(ks[8], (N_DENSE_LAYERS, D, D_FF_DENSE), D)
    ffn_w_down = nrm(ks[9], (N_DENSE_LAYERS, D_FF_DENSE, D), D_FF_DENSE)
    moe_w_router = nrm(ks[10], (N_MOE_LAYERS, D, N_EXPERTS), D)
    moe_w_gate = nrm(ks[11], (N_MOE_LAYERS, N_EXPERTS, D, D_FF_EXPERT), D)
    moe_w_up = nrm(ks[12], (N_MOE_LAYERS, N_EXPERTS, D, D_FF_EXPERT), D)
    moe_w_down = nrm(ks[13], (N_MOE_LAYERS, N_EXPERTS, D_FF_EXPERT, D), D_FF_EXPERT)
    return {"x": x, "norm_g": norm_g, "pool_w": pool_w, "pool_scale": pool_scale,
            "attn_w_in": attn_w_in, "attn_b_f": attn_b_f, "attn_w_out": attn_w_out,
            "ffn_w_gate": ffn_w_gate, "ffn_w_up": ffn_w_up, "ffn_w_down": ffn_w_down,
            "moe_w_router": moe_w_router, "moe_w_gate": moe_w_gate,
            "moe_w_up": moe_w_up, "moe_w_down": moe_w_down}


def reference(x, norm_g, pool_w, pool_scale, attn_w_in, attn_b_f, attn_w_out,
              ffn_w_gate, ffn_w_up, ffn_w_down,
              moe_w_router, moe_w_gate, moe_w_up, moe_w_down):
    for i in range(DEPTH):
        slot = i // 2
        h = rms_norm(x, norm_g[i, 0])
        if i % N_MIXERS == 0:
            y = pool_mixer(h, pool_w[slot], pool_scale[slot])
        else:
            y = forgetting_attention(h, attn_w_in[slot], attn_b_f[slot], attn_w_out[slot])
        x = x + rms_norm(y, norm_g[i, 1])
        h = rms_norm(x, norm_g[i, 2])
        if i % 2 == 0:
            y = swiglu(h, ffn_w_gate[slot], ffn_w_up[slot], ffn_w_down[slot])
        else:
            y = moe_swiglu(h, moe_w_router[slot], moe_w_gate[slot], moe_w_up[slot], moe_w_down[slot])
        x = x + rms_norm(y, norm_g[i, 3])
    return x
```

```python
import functools

import jax
import jax.numpy as jnp
from jax import lax
from jax.experimental import pallas as pl
from jax.experimental.pallas import tpu as pltpu

F32 = jnp.float32
BF16 = jnp.bfloat16

RMS_EPS = 1e-6
HEAD_DIM = 64
POOL_WINDOWS = (2, 4, 8, 16)
POOL_HALO = 16
N_EXPERTS = 8
NEG_INF = -1e30
LANES = 128
VMEM_LIMIT = 56 * 1024 * 1024


def _rms(x, g):
    ms = jnp.mean(x * x, axis=-1, keepdims=True)
    return x * lax.rsqrt(ms + RMS_EPS) * g


def _split3(c):
    hi = c.astype(BF16).astype(F32)
    r = c - hi
    mid = r.astype(BF16).astype(F32)
    lo = r - mid
    return hi, mid, lo


def _resident(shape):
    nd = len(shape)
    return pl.BlockSpec(shape, lambda *_: (0,) * nd, pipeline_mode=pl.Buffered(1))


def _layer0_kernel(x_ref, xp_ref, g_ref, pw_ref, ps_ref, wg_ref, wu_ref, wd_ref, o_ref,
                   buf_a, buf_b, *, tm, ff_chunk):
    s = pl.program_id(1)
    d = x_ref.shape[-1]
    gd = d // len(POOL_WINDOWS)
    x = x_ref[0]
    g = g_ref[...]
    h = _rms(x, g[0:1])
    hp = _rms(xp_ref[0], g[0:1])
    hp = jnp.where(s > 0, hp, 0.0)

    lo = 8
    top = tm + POOL_HALO + lo
    zeros8 = jnp.zeros((lo, d), F32)
    buf_a[0:lo, :] = zeros8
    buf_b[0:lo, :] = zeros8
    buf_a[lo:lo + POOL_HALO, :] = hp
    buf_a[lo + POOL_HALO:top, :] = h
    body = lo + POOL_HALO

    def shifted_sum(src, k, c0, rows0):
        n = top - rows0
        return src[rows0:top, c0:] + src[rows0 - k:rows0 - k + n, c0:]

    sums = []
    sums.append((buf_a[body:top, 0:gd] + buf_a[body - 1:top - 1, 0:gd]))
    buf_b[lo:top, gd:] = shifted_sum(buf_a, 1, gd, lo)
    sums.append(buf_b[body:top, gd:2 * gd] + buf_b[body - 2:top - 2, gd:2 * gd])
    buf_a[lo:top, 2 * gd:] = shifted_sum(buf_b, 2, 2 * gd, lo)
    sums.append(buf_a[body:top, 2 * gd:3 * gd] + buf_a[body - 4:top - 4, 2 * gd:3 * gd])
    buf_b[lo:top, 3 * gd:] = shifted_sum(buf_a, 4, 3 * gd, lo)
    sums.append(buf_b[body:top, 3 * gd:] + buf_b[body - 8:top - 8, 3 * gd:])

    pos = s * tm + lax.broadcasted_iota(jnp.int32, (tm, 1), 0)
    mixed = []
    for gi, w in enumerate(POOL_WINDOWS):
        count = jnp.minimum(pos + 1, w).astype(F32)
        pooled = sums[gi] / count - h[:, gi * gd:(gi + 1) * gd]
        mixed.append(jnp.dot(pooled.astype(BF16), pw_ref[gi], preferred_element_type=F32))
    y = jnp.concatenate(mixed, axis=-1) * ps_ref[...]
    x1 = x + _rms(y, g[1:2])

    h2 = _rms(x1, g[2:3]).astype(BF16)
    dff = wg_ref.shape[1]
    acc = jnp.zeros((tm, d), F32)
    for c0 in range(0, dff, ff_chunk):
        gate = jnp.dot(h2, wg_ref[:, c0:c0 + ff_chunk], preferred_element_type=F32)
        up = jnp.dot(h2, wu_ref[:, c0:c0 + ff_chunk], preferred_element_type=F32)
        act = (gate * jax.nn.sigmoid(gate) * up).astype(BF16)
        acc = acc + jnp.dot(act, wd_ref[c0:c0 + ff_chunk, :], preferred_element_type=F32)
    o_ref[0] = x1 + _rms(acc, g[3:4])


def _layer0(x, g4, pool_w, pool_scale, w_gate, w_up, w_down, *, tm):
    b, s, d = x.shape
    dff = w_gate.shape[1]
    ff_chunk = dff // 2 if (dff // 2) % LANES == 0 else dff
    halo_per_tile = tm // POOL_HALO
    kern = functools.partial(_layer0_kernel, tm=tm, ff_chunk=ff_chunk)
    return pl.pallas_call(
        kern,
        grid=(b, s // tm),
        in_specs=[
            pl.BlockSpec((1, tm, d), lambda bi, si: (bi, si, 0)),
            pl.BlockSpec((1, POOL_HALO, d), lambda bi, si: (bi, jnp.maximum(si * halo_per_tile - 1, 0), 0)),
            _resident((4, d)),
            _resident(pool_w.shape),
            _resident((1, d)),
            _resident(w_gate.shape),
            _resident(w_up.shape),
            _resident(w_down.shape),
        ],
        out_specs=pl.BlockSpec((1, tm, d), lambda bi, si: (bi, si, 0)),
        out_shape=jax.ShapeDtypeStruct((b, s, d), F32),
        scratch_shapes=[pltpu.VMEM((tm + POOL_HALO + 8, d), F32),
                        pltpu.VMEM((tm + POOL_HALO + 8, d), F32)],
        compiler_params=pltpu.CompilerParams(
            dimension_semantics=("arbitrary", "arbitrary"), vmem_limit_bytes=VMEM_LIMIT),
        name="layer0",
    )(x, x, g4, pool_w, pool_scale.reshape(1, d), w_gate, w_up, w_down)


def _qkv_kernel(x_ref, g_ref, wqkv_ref, wf_ref, bf_ref, tri_ref, q_ref, k_ref, v_ref, carry_ref,
                *, tm, n_heads):
    s = pl.program_id(1)
    d = x_ref.shape[-1]
    h = _rms(x_ref[0], g_ref[...]).astype(BF16)
    proj = jnp.dot(h, wqkv_ref[...], preferred_element_type=F32)
    z = jnp.dot(h, wf_ref[...], preferred_element_type=F32) + bf_ref[...]
    log_f = jnp.minimum(z, 0.0) - jnp.log(1.0 + jnp.exp(-jnp.abs(z)))

    @pl.when(s == 0)
    def _():
        carry_ref[...] = jnp.zeros_like(carry_ref)

    tri = tri_ref[...]
    c = carry_ref[...]
    for piece in _split3(log_f):
        c = c + jnp.dot(tri, piece.astype(BF16), preferred_element_type=F32)
    carry_ref[...] = c[tm - 1:tm, :]
    c_hi, c_mid, c_lo = _split3(c)

    lane = lax.broadcasted_iota(jnp.int32, (tm, HEAD_DIM), 1)
    scale = HEAD_DIM ** -0.5
    for hd in range(n_heads):
        chi = c_hi[:, hd:hd + 1]
        cmid = c_mid[:, hd:hd + 1]
        clo = c_lo[:, hd:hd + 1]
        aug_q = jnp.where(lane == 0, chi, jnp.where(lane == 1, cmid, jnp.where(lane == 2, clo,
                          jnp.where(lane < 6, 1.0, 0.0))))
        aug_k = jnp.where(lane == 3, -chi, jnp.where(lane == 4, -cmid, jnp.where(lane == 5, -clo,
                          jnp.where(lane < 3, 1.0, 0.0))))
        aug_v = jnp.where(lane == 0, 1.0, 0.0)
        c0 = hd * HEAD_DIM
        qh = proj[:, c0:c0 + HEAD_DIM] * scale
        kh = proj[:, d + c0:d + c0 + HEAD_DIM]
        vh = proj[:, 2 * d + c0:2 * d + c0 + HEAD_DIM]
        q_ref[0, hd] = jnp.concatenate([qh, aug_q], axis=-1).astype(BF16)
        k_ref[0, hd] = jnp.concatenate([kh, aug_k], axis=-1).astype(BF16)
        v_ref[0, hd] = jnp.concatenate([vh, aug_v], axis=-1).astype(BF16)


def _qkv(x, g, w_qkv, w_f, b_f, *, tm):
    b, s, d = x.shape
    n_heads = d // HEAD_DIM
    tri = jnp.tril(jnp.ones((tm, tm), BF16))
    kern = functools.partial(_qkv_kernel, tm=tm, n_heads=n_heads)
    out_sds = jax.ShapeDtypeStruct((b, n_heads, s, 2 * HEAD_DIM), BF16)
    out_spec = pl.BlockSpec((1, n_heads, tm, 2 * HEAD_DIM), lambda bi, si: (bi, 0, si, 0))
    return pl.pallas_call(
        kern,
        grid=(b, s // tm),
        in_specs=[
            pl.BlockSpec((1, tm, d), lambda bi, si: (bi, si, 0)),
            _resident((1, d)),
            _resident(w_qkv.shape),
            _resident(w_f.shape),
            _resident(b_f.shape),
            _resident((tm, tm)),
        ],
        out_specs=[out_spec, out_spec, out_spec],
        out_shape=[out_sds, out_sds, out_sds],
        scratch_shapes=[pltpu.VMEM((1, LANES), F32)],
        compiler_params=pltpu.CompilerParams(
            dimension_semantics=("arbitrary", "arbitrary"), vmem_limit_bytes=VMEM_LIMIT),
        name="qkv",
    )(x, g, w_qkv, w_f, b_f, tri)


def _attn_kernel(q_ref, k_ref, v_ref, o_ref, *, tq, heads_per_step):
    qi = pl.program_id(2)
    outs = []
    for hh in range(heads_per_step):
        q = q_ref[0, hh]

        def scores(j):
            k = k_ref[0, hh, pl.ds(pl.multiple_of(j * tq, tq), tq), :]
            return lax.dot_general(q, k, (((1,), (1,)), ((), ())), preferred_element_type=F32)

        def update(j, sc, m, acc):
            v = v_ref[0, hh, pl.ds(pl.multiple_of(j * tq, tq), tq), :]
            m_new = jnp.maximum(m, jnp.max(sc, axis=-1, keepdims=True))
            p = jnp.exp(sc - m_new)
            alpha = jnp.exp(m - m_new)
            acc = alpha * acc + jnp.dot(p.astype(BF16), v, preferred_element_type=F32)
            return m_new, acc

        def step(j, carry):
            m, acc = carry
            return update(j, scores(j), m, acc)

        m0 = jnp.full((tq, 1), NEG_INF, F32)
        acc0 = jnp.zeros((tq, 2 * HEAD_DIM), F32)
        m, acc = lax.fori_loop(0, qi, step, (m0, acc0))
        row = lax.broadcasted_iota(jnp.int32, (tq, tq), 0)
        col = lax.broadcasted_iota(jnp.int32, (tq, tq), 1)
        sc = jnp.where(row >= col, scores(qi), NEG_INF)
        m, acc = update(qi, sc, m, acc)
        outs.append(acc[:, :HEAD_DIM] / acc[:, HEAD_DIM:HEAD_DIM + 1])
    o_ref[0] = jnp.concatenate(outs, axis=-1).astype(o_ref.dtype)


def _attention(q, k, v, *, tq):
    b, nh, s, dk = q.shape
    hps = LANES // HEAD_DIM
    kern = functools.partial(_attn_kernel, tq=tq, heads_per_step=hps)
    return pl.pallas_call(
        kern,
        grid=(b, nh // hps, s // tq),
        in_specs=[
            pl.BlockSpec((1, hps, tq, dk), lambda bi, hi, qi: (bi, hi, qi, 0)),
            pl.BlockSpec((1, hps, s, dk), lambda bi, hi, qi: (bi, hi, 0, 0)),
            pl.BlockSpec((1, hps, s, dk), lambda bi, hi, qi: (bi, hi, 0, 0)),
        ],
        out_specs=pl.BlockSpec((1, tq, LANES), lambda bi, hi, qi: (bi, qi, hi)),
        out_shape=jax.ShapeDtypeStruct((b, s, nh * HEAD_DIM), BF16),
        compiler_params=pltpu.CompilerParams(
            dimension_semantics=("arbitrary", "arbitrary", "arbitrary"), vmem_limit_bytes=VMEM_LIMIT),
        name="attention",
    )(q, k, v)


def _attn_out_kernel(o_ref, x_ref, g_ref, wo_ref, wr_ref, ltri_ref, x3_ref, ri_ref, rw_ref, cnt_ref,
                     carry_ref, *, tm):
    i = pl.program_id(0)
    g = g_ref[...]
    y = jnp.dot(o_ref[...], wo_ref[...], preferred_element_type=F32)
    x3 = x_ref[...] + _rms(y, g[0:1])
    x3_ref[...] = x3
    h = _rms(x3, g[1:2])
    logits = jnp.dot(h, wr_ref[...], preferred_element_type=F32, precision=lax.Precision.HIGHEST)

    lane = lax.broadcasted_iota(jnp.int32, (tm, LANES), 1)
    valid = lane < N_EXPERTS
    logits = jnp.where(valid, logits, -jnp.inf)
    m1 = jnp.max(logits, axis=-1, keepdims=True)
    e1 = jnp.min(jnp.where(logits == m1, lane, LANES), axis=-1, keepdims=True)
    rest = jnp.where(lane == e1, -jnp.inf, logits)
    m2 = jnp.max(rest, axis=-1, keepdims=True)
    e2 = jnp.min(jnp.where(rest == m2, lane, LANES), axis=-1, keepdims=True)
    t = jnp.exp(m2 - m1)
    w1 = 1.0 / (1.0 + t)
    w2 = t / (1.0 + t)

    @pl.when(i == 0)
    def _():
        carry_ref[...] = jnp.zeros_like(carry_ref)

    hot1 = lane == e1
    hot2 = lane == e2
    cnt = jnp.where(hot1 | hot2, 1.0, 0.0)
    before = jnp.dot(ltri_ref[...], cnt.astype(BF16), preferred_element_type=F32) + carry_ref[...]
    r1 = jnp.sum(jnp.where(hot1, before, 0.0), axis=-1, keepdims=True)
    r2 = jnp.sum(jnp.where(hot2, before, 0.0), axis=-1, keepdims=True)
    carry_ref[...] = carry_ref[...] + jnp.sum(cnt, axis=0, keepdims=True)
    cnt_ref[...] = jnp.broadcast_to(carry_ref[...], cnt_ref.shape).astype(jnp.int32)

    ri_ref[...] = jnp.where(lane == 0, e1, jnp.where(lane == 1, e2, jnp.where(
        lane == 2, r1.astype(jnp.int32), jnp.where(lane == 3, r2.astype(jnp.int32), 0))))
    rw_ref[...] = jnp.where(lane == 0, w1, jnp.where(lane == 1, w2, 0.0))


def _attn_out(o, x, g2, w_out, w_router_pad, *, tm):
    n, d = x.shape
    ltri = jnp.tril(jnp.ones((tm, tm), BF16), k=-1)
    kern = functools.partial(_attn_out_kernel, tm=tm)
    row_spec = pl.BlockSpec((tm, d), lambda i: (i, 0))
    meta_spec = pl.BlockSpec((tm, LANES), lambda i: (i, 0))
    return pl.pallas_call(
        kern,
        grid=(n // tm,),
        in_specs=[row_spec, row_spec, _resident((2, d)), _resident(w_out.shape),
                  _resident(w_router_pad.shape), _resident((tm, tm))],
        out_specs=[row_spec, meta_spec, meta_spec, pl.BlockSpec((8, LANES), lambda i: (0, 0))],
        out_shape=[jax.ShapeDtypeStruct((n, d), F32),
                   jax.ShapeDtypeStruct((n, LANES), jnp.int32),
                   jax.ShapeDtypeStruct((n, LANES), F32),
                   jax.ShapeDtypeStruct((8, LANES), jnp.int32)],
        scratch_shapes=[pltpu.VMEM((1, LANES), F32)],
        compiler_params=pltpu.CompilerParams(
            dimension_semantics=("arbitrary",), vmem_limit_bytes=VMEM_LIMIT),
        name="attn_out",
    )(o, x, g2, w_out, w_router_pad, ltri)


def _row_copy(src, dst, sem):
    return pltpu.make_async_copy(src, dst, sem)


def _dispatch_kernel(fill_ref, pos_ref, x_ref, g_ref, xs_ref, h_buf, z_buf, sem, *, tb):
    i = pl.program_id(0)
    h_buf[...] = _rms(x_ref[...], g_ref[...])

    def issue(t, _):
        for slot in range(2):
            p = pos_ref[0, slot, t]
            _row_copy(h_buf.at[pl.ds(t, 1), :], xs_ref.at[pl.ds(p, 1), :], sem).start()
        return 0

    lax.fori_loop(0, tb, issue, 0)

    @pl.when(i == 0)
    def _():
        z_buf[...] = jnp.zeros_like(z_buf)
        for e in range(N_EXPERTS):
            lo = fill_ref[0, e]
            hi = fill_ref[1, e]

            def fill(r, _):
                _row_copy(z_buf, xs_ref.at[pl.ds(r, 1), :], sem).start()
                return 0

            lax.fori_loop(lo, hi, fill, 0)

            def drain(r, _):
                _row_copy(z_buf, xs_ref.at[pl.ds(0, 1), :], sem).wait()
                return 0

            lax.fori_loop(lo, hi, drain, 0)

    def drain_rows(t, _):
        _row_copy(h_buf.at[pl.ds(0, 1), :], xs_ref.at[pl.ds(0, 1), :], sem).wait()
        return 0

    lax.fori_loop(0, 2 * tb, drain_rows, 0)


def _dispatch(x3, g, pos, fill, n_rows, *, tb):
    n, d = x3.shape
    kern = functools.partial(_dispatch_kernel, tb=tb)
    return pl.pallas_call(
        kern,
        grid_spec=pltpu.PrefetchScalarGridSpec(
            num_scalar_prefetch=1,
            grid=(n // tb,),
            in_specs=[
                pl.BlockSpec((1, 2, tb), lambda i, f: (i, 0, 0), memory_space=pltpu.SMEM),
                pl.BlockSpec((tb, d), lambda i, f: (i, 0)),
                pl.BlockSpec((1, d), lambda i, f: (0, 0)),
            ],
            out_specs=pl.BlockSpec(memory_space=pl.ANY),
            scratch_shapes=[pltpu.VMEM((tb, d), F32), pltpu.VMEM((1, d), F32),
                            pltpu.SemaphoreType.DMA(())],
        ),
        out_shape=jax.ShapeDtypeStruct((n_rows, d), F32),
        compiler_params=pltpu.CompilerParams(
            dimension_semantics=("arbitrary",), vmem_limit_bytes=VMEM_LIMIT),
        name="dispatch",
    )(fill, pos, x3, g)


def _experts_kernel(te_ref, nt_ref, xs_ref, wg_ref, wu_ref, wd_ref, ys_ref, xb_ref, acc_ref):
    i = pl.program_id(0)
    j = pl.program_id(1)
    nj = pl.num_programs(1)

    @pl.when(i < nt_ref[0])
    def _():
        @pl.when(j == 0)
        def _():
            xb_ref[...] = xs_ref[...].astype(BF16)
            acc_ref[...] = jnp.zeros_like(acc_ref)

        xb = xb_ref[...]
        gate = jnp.dot(xb, wg_ref[0], preferred_element_type=F32)
        up = jnp.dot(xb, wu_ref[0], preferred_element_type=F32)
        act = (gate * jax.nn.sigmoid(gate) * up).astype(BF16)
        acc_ref[...] += jnp.dot(act, wd_ref[0], preferred_element_type=F32)

        @pl.when(j == nj - 1)
        def _():
            ys_ref[...] = acc_ref[...]

    @pl.when((i >= nt_ref[0]) & (j == nj - 1))
    def _():
        ys_ref[...] = jnp.zeros_like(ys_ref)


def _experts(xs, tile_expert, n_tiles, w_gate, w_up, w_down, *, tm, tf):
    n_rows, d = xs.shape
    dff = w_gate.shape[-1]
    max_tiles = n_rows // tm

    def row_map(i, j, te, nt):
        return (jnp.minimum(i, nt[0] - 1), 0)

    def wcol_map(i, j, te, nt):
        return (te[i], 0, j)

    def wrow_map(i, j, te, nt):
        return (te[i], j, 0)

    return pl.pallas_call(
        _experts_kernel,
        grid_spec=pltpu.PrefetchScalarGridSpec(
            num_scalar_prefetch=2,
            grid=(max_tiles, dff // tf),
            in_specs=[
                pl.BlockSpec((tm, d), row_map),
                pl.BlockSpec((1, d, tf), wcol_map),
                pl.BlockSpec((1, d, tf), wcol_map),
                pl.BlockSpec((1, tf, d), wrow_map),
            ],
            out_specs=pl.BlockSpec((tm, d), lambda i, j, te, nt: (i, 0)),
            scratch_shapes=[pltpu.VMEM((tm, d), BF16), pltpu.VMEM((tm, d), F32)],
        ),
        out_shape=jax.ShapeDtypeStruct((n_rows, d), F32),
        compiler_params=pltpu.CompilerParams(
            dimension_semantics=("arbitrary", "arbitrary"), vmem_limit_bytes=VMEM_LIMIT),
        name="experts",
    )(tile_expert, n_tiles, xs, w_gate, w_up, w_down)


def _combine_kernel(pos_ref, ys_ref, x_ref, rw_ref, g_ref, o_ref, a_buf, b_buf, sem, *, tb):
    def issue(t, _):
        _row_copy(ys_ref.at[pl.ds(pos_ref[0, 0, t], 1), :], a_buf.at[pl.ds(t, 1), :], sem).start()
        _row_copy(ys_ref.at[pl.ds(pos_ref[0, 1, t], 1), :], b_buf.at[pl.ds(t, 1), :], sem).start()
        return 0

    lax.fori_loop(0, tb, issue, 0)

    def drain(t, _):
        _row_copy(ys_ref.at[pl.ds(0, 1), :], a_buf.at[pl.ds(0, 1), :], sem).wait()
        return 0

    lax.fori_loop(0, 2 * tb, drain, 0)
    rw = rw_ref[...]
    y = rw[:, 0:1] * a_buf[...] + rw[:, 1:2] * b_buf[...]
    o_ref[...] = x_ref[...] + _rms(y, g_ref[...])


def _combine(ys, x3, rw, g, pos, *, tb):
    n, d = x3.shape
    kern = functools.partial(_combine_kernel, tb=tb)
    return pl.pallas_call(
        kern,
        grid=(n // tb,),
        in_specs=[
            pl.BlockSpec((1, 2, tb), lambda i: (i, 0, 0), memory_space=pltpu.SMEM),
            pl.BlockSpec(memory_space=pl.ANY),
            pl.BlockSpec((tb, d), lambda i: (i, 0)),
            pl.BlockSpec((tb, LANES), lambda i: (i, 0)),
            pl.BlockSpec((1, d), lambda i: (0, 0)),
        ],
        out_specs=pl.BlockSpec((tb, d), lambda i: (i, 0)),
        out_shape=jax.ShapeDtypeStruct((n, d), F32),
        scratch_shapes=[pltpu.VMEM((tb, d), F32), pltpu.VMEM((tb, d), F32),
                        pltpu.SemaphoreType.DMA(())],
        compiler_params=pltpu.CompilerParams(
            dimension_semantics=("arbitrary",), vmem_limit_bytes=VMEM_LIMIT),
        name="combine",
    )(pos, ys, x3, rw, g)


def _pick_tile(n, pref):
    t = min(pref, n)
    while n % t:
        t //= 2
    return t


def kernel(x, norm_g, pool_w, pool_scale, attn_w_in, attn_b_f, attn_w_out, ffn_w_gate, ffn_w_up, ffn_w_down,
           moe_w_router, moe_w_gate, moe_w_up, moe_w_down):
    b, s, d = x.shape
    n = b * s
    nh = d // HEAD_DIM
    tm = _pick_tile(s, 512)

    x = _layer0(x, norm_g[0], pool_w[0].astype(BF16), pool_scale[0],
                ffn_w_gate[0].astype(BF16), ffn_w_up[0].astype(BF16), ffn_w_down[0].astype(BF16), tm=tm)

    w_in = attn_w_in[0]
    w_f = jnp.zeros((d, LANES), F32).at[:, :nh].set(w_in[:, 3 * d:]).astype(BF16)
    b_f = jnp.zeros((1, LANES), F32).at[0, :nh].set(attn_b_f[0])
    q, k, v = _qkv(x, norm_g[1, 0:1], w_in[:, :3 * d].astype(BF16), w_f, b_f, tm=tm)
    o = _attention(q, k, v, tq=tm)

    w_router = jnp.zeros((d, LANES), F32).at[:, :N_EXPERTS].set(moe_w_router[0])
    x3, ri, rw, counts = _attn_out(o.reshape(n, d), x.reshape(n, d), norm_g[1, 1:3],
                                   attn_w_out[0].astype(BF16), w_router, tm=tm)

    tme = _pick_tile(n, 512)
    counts = counts[0, :N_EXPERTS]
    padded = (counts + tme - 1) // tme * tme
    ends = jnp.cumsum(padded)
    starts = ends - padded
    n_rows = 2 * n + N_EXPERTS * tme
    pos = jnp.stack([starts[ri[:, 0]] + ri[:, 2], starts[ri[:, 1]] + ri[:, 3]], axis=0)
    tb = _pick_tile(n, 256)
    pos = pos.reshape(2, n // tb, tb).transpose(1, 0, 2)
    fill_hi = ends.at[N_EXPERTS - 1].set(n_rows)
    fill = jnp.stack([starts + counts, fill_hi], axis=0).astype(jnp.int32)
    tile_start = jnp.arange(n_rows // tme, dtype=jnp.int32) * tme
    tile_expert = jnp.minimum(jnp.searchsorted(ends, tile_start, side="right"), N_EXPERTS - 1).astype(jnp.int32)
    n_tiles = (ends[-1:] // tme).astype(jnp.int32)

    xs = _dispatch(x3, norm_g[1, 2:3], pos, fill, n_rows, tb=tb)
    dffe = moe_w_gate.shape[-1]
    tf = dffe // 2 if (dffe // 2) % LANES == 0 else dffe
    ys = _experts(xs, tile_expert, n_tiles, moe_w_gate[0].astype(BF16), moe_w_up[0].astype(BF16),
                  moe_w_down[0].astype(BF16), tm=tme, tf=tf)
    out = _combine(ys, x3, rw, norm_g[1, 3:4], pos, tb=tb)
    return out.reshape(b, s, d)
```

```python
import functools

import jax
import jax.numpy as jnp
from jax import lax
from jax.experimental import pallas as pl
from jax.experimental.pallas import tpu as pltpu

F32 = jnp.float32
BF16 = jnp.bfloat16

RMS_EPS = 1e-6
HEAD_DIM = 64
POOL_WINDOWS = (2, 4, 8, 16)
POOL_HALO = 16
N_EXPERTS = 8
NEG_INF = -1e30
LANES = 128
VMEM_LIMIT = 56 * 1024 * 1024
ISSUE_UNROLL = 8
LOG2E = 1.4426950408889634


def _rms(x, g):
    ms = jnp.mean(x * x, axis=-1, keepdims=True)
    return x * lax.rsqrt(ms + RMS_EPS) * g


def _split3(c):
    hi = c.astype(BF16).astype(F32)
    r = c - hi
    mid = r.astype(BF16).astype(F32)
    lo = r - mid
    return hi, mid, lo


def _resident(shape):
    nd = len(shape)
    return pl.BlockSpec(shape, lambda *_: (0,) * nd, pipeline_mode=pl.Buffered(1))


def _layer0_kernel(x_ref, xp_ref, g_ref, pw_ref, ps_ref, wg_ref, wu_ref, wd_ref, o_ref,
                   buf_a, buf_b, *, tm, ff_chunk):
    s = pl.program_id(1)
    d = x_ref.shape[-1]
    gd = d // len(POOL_WINDOWS)
    x = x_ref[0]
    g = g_ref[...]
    h = _rms(x, g[0:1])
    hp = _rms(xp_ref[0], g[0:1])
    hp = jnp.where(s > 0, hp, 0.0)

    lo = 8
    top = tm + POOL_HALO + lo
    zeros8 = jnp.zeros((lo, d), F32)
    buf_a[0:lo, :] = zeros8
    buf_b[0:lo, :] = zeros8
    buf_a[lo:lo + POOL_HALO, :] = hp
    buf_a[lo + POOL_HALO:top, :] = h
    body = lo + POOL_HALO

    def shifted_sum(src, k, c0, rows0):
        n = top - rows0
        return src[rows0:top, c0:] + src[rows0 - k:rows0 - k + n, c0:]

    sums = []
    sums.append((buf_a[body:top, 0:gd] + buf_a[body - 1:top - 1, 0:gd]))
    buf_b[lo:top, gd:] = shifted_sum(buf_a, 1, gd, lo)
    sums.append(buf_b[body:top, gd:2 * gd] + buf_b[body - 2:top - 2, gd:2 * gd])
    buf_a[lo:top, 2 * gd:] = shifted_sum(buf_b, 2, 2 * gd, lo)
    sums.append(buf_a[body:top, 2 * gd:3 * gd] + buf_a[body - 4:top - 4, 2 * gd:3 * gd])
    buf_b[lo:top, 3 * gd:] = shifted_sum(buf_a, 4, 3 * gd, lo)
    sums.append(buf_b[body:top, 3 * gd:] + buf_b[body - 8:top - 8, 3 * gd:])

    pos = s * tm + lax.broadcasted_iota(jnp.int32, (tm, 1), 0)
    mixed = []
    for gi, w in enumerate(POOL_WINDOWS):
        count = jnp.minimum(pos + 1, w).astype(F32)
        pooled = sums[gi] / count - h[:, gi * gd:(gi + 1) * gd]
        mixed.append(jnp.dot(pooled.astype(BF16), pw_ref[gi], preferred_element_type=F32))
    y = jnp.concatenate(mixed, axis=-1) * ps_ref[...]
    x1 = x + _rms(y, g[1:2])

    h2 = _rms(x1, g[2:3]).astype(BF16)
    dff = wg_ref.shape[1]
    acc = jnp.zeros((tm, d), F32)
    for c0 in range(0, dff, ff_chunk):
        gate = jnp.dot(h2, wg_ref[:, c0:c0 + ff_chunk], preferred_element_type=F32)
        up = jnp.dot(h2, wu_ref[:, c0:c0 + ff_chunk], preferred_element_type=F32)
        act = (gate * jax.nn.sigmoid(gate) * up).astype(BF16)
        acc = acc + jnp.dot(act, wd_ref[c0:c0 + ff_chunk, :], preferred_element_type=F32)
    o_ref[0] = x1 + _rms(acc, g[3:4])


def _layer0(x, g4, pool_w, pool_scale, w_gate, w_up, w_down, *, tm):
    b, s, d = x.shape
    dff = w_gate.shape[1]
    ff_chunk = dff // 2 if (dff // 2) % LANES == 0 else dff
    halo_per_tile = tm // POOL_HALO
    kern = functools.partial(_layer0_kernel, tm=tm, ff_chunk=ff_chunk)
    return pl.pallas_call(
        kern,
        grid=(b, s // tm),
        in_specs=[
            pl.BlockSpec((1, tm, d), lambda bi, si: (bi, si, 0)),
            pl.BlockSpec((1, POOL_HALO, d), lambda bi, si: (bi, jnp.maximum(si * halo_per_tile - 1, 0), 0)),
            _resident((4, d)),
            _resident(pool_w.shape),
            _resident((1, d)),
            _resident(w_gate.shape),
            _resident(w_up.shape),
            _resident(w_down.shape),
        ],
        out_specs=pl.BlockSpec((1, tm, d), lambda bi, si: (bi, si, 0)),
        out_shape=jax.ShapeDtypeStruct((b, s, d), F32),
        scratch_shapes=[pltpu.VMEM((tm + POOL_HALO + 8, d), F32),
                        pltpu.VMEM((tm + POOL_HALO + 8, d), F32)],
        compiler_params=pltpu.CompilerParams(
            dimension_semantics=("arbitrary", "arbitrary"), vmem_limit_bytes=VMEM_LIMIT),
        name="layer0",
    )(x, x, g4, pool_w, pool_scale.reshape(1, d), w_gate, w_up, w_down)


def _qkv_kernel(x_ref, g_ref, wk_ref, wqvt_ref, wf_ref, bf_ref, tri_ref, qt_ref, k_ref, vt_ref, carry_ref,
                *, tm, n_heads):
    s = pl.program_id(1)
    d = x_ref.shape[-1]
    h = _rms(x_ref[0], g_ref[...]).astype(BF16)
    nt = (((1,), (1,)), ((), ()))
    kproj = jnp.dot(h, wk_ref[...], preferred_element_type=F32)
    qvt = lax.dot_general(wqvt_ref[...], h, nt, preferred_element_type=F32)
    z = jnp.dot(h, wf_ref[...], preferred_element_type=F32) + bf_ref[...]
    log_f = jnp.minimum(z, 0.0) - jnp.log(1.0 + jnp.exp(-jnp.abs(z)))

    @pl.when(s == 0)
    def _():
        carry_ref[...] = jnp.zeros_like(carry_ref)

    tri = tri_ref[...]
    c = carry_ref[...]
    for piece in _split3(log_f):
        c = c + jnp.dot(tri, piece.astype(BF16), preferred_element_type=F32)
    carry_ref[...] = c[tm - 1:tm, :]
    c = c * LOG2E
    c_hi, c_mid, c_lo = _split3(c)
    ct_hi, ct_mid, ct_lo = _split3(c.T)

    lane = lax.broadcasted_iota(jnp.int32, (tm, HEAD_DIM), 1)
    row = lax.broadcasted_iota(jnp.int32, (HEAD_DIM, tm), 0)
    aug_vt = jnp.where(row == 0, 1.0, 0.0)
    scale = HEAD_DIM ** -0.5 * LOG2E
    for hd in range(n_heads):
        aug_k = jnp.where(lane == 3, -c_hi[:, hd:hd + 1], jnp.where(
            lane == 4, -c_mid[:, hd:hd + 1], jnp.where(
                lane == 5, -c_lo[:, hd:hd + 1], jnp.where(lane < 3, 1.0, 0.0))))
        aug_qt = jnp.where(row == 0, ct_hi[hd:hd + 1, :], jnp.where(
            row == 1, ct_mid[hd:hd + 1, :], jnp.where(
                row == 2, ct_lo[hd:hd + 1, :], jnp.where(row < 6, 1.0, 0.0))))
        c0 = hd * HEAD_DIM
        k_ref[0, hd] = jnp.concatenate([kproj[:, c0:c0 + HEAD_DIM], aug_k], axis=-1).astype(BF16)
        qt_ref[0, hd] = jnp.concatenate([qvt[c0:c0 + HEAD_DIM, :] * scale, aug_qt], axis=0).astype(BF16)
        vt_ref[0, hd] = jnp.concatenate([qvt[d + c0:d + c0 + HEAD_DIM, :], aug_vt], axis=0).astype(BF16)


def _qkv(x, g, w_k, w_qvt, w_f, b_f, *, tm):
    b, s, d = x.shape
    n_heads = d // HEAD_DIM
    dk = 2 * HEAD_DIM
    tri = jnp.tril(jnp.ones((tm, tm), BF16))
    kern = functools.partial(_qkv_kernel, tm=tm, n_heads=n_heads)
    row_sds = jax.ShapeDtypeStruct((b, n_heads, s, dk), BF16)
    col_sds = jax.ShapeDtypeStruct((b, n_heads, dk, s), BF16)
    row_spec = pl.BlockSpec((1, n_heads, tm, dk), lambda bi, si: (bi, 0, si, 0))
    col_spec = pl.BlockSpec((1, n_heads, dk, tm), lambda bi, si: (bi, 0, 0, si))
    return pl.pallas_call(
        kern,
        grid=(b, s // tm),
        in_specs=[
            pl.BlockSpec((1, tm, d), lambda bi, si: (bi, si, 0)),
            _resident((1, d)),
            _resident(w_k.shape),
            _resident(w_qvt.shape),
            _resident(w_f.shape),
            _resident(b_f.shape),
            _resident((tm, tm)),
        ],
        out_specs=[col_spec, row_spec, col_spec],
        out_shape=[col_sds, row_sds, col_sds],
        scratch_shapes=[pltpu.VMEM((1, LANES), F32)],
        compiler_params=pltpu.CompilerParams(
            dimension_semantics=("arbitrary", "arbitrary"), vmem_limit_bytes=VMEM_LIMIT),
        name="qkv",
    )(x, g, w_k, w_qvt, w_f, b_f, tri)


def _attn_kernel(qt_ref, k_ref, vt_ref, o_ref, m_ref, acc_ref, s0_ref, s1_ref, *, tq, tk, heads_per_step):
    qi = pl.program_id(2)
    m_ref[...] = jnp.full(m_ref.shape, NEG_INF, F32)
    acc_ref[...] = jnp.zeros(acc_ref.shape, F32)

    def scores(sub, dst_ref):
        start = pl.multiple_of(sub * tk, tk)
        for hh in range(heads_per_step):
            k = k_ref[0, hh, pl.ds(start, tk), :]
            dst_ref[hh] = jnp.dot(k, qt_ref[0, hh], preferred_element_type=F32)

    def consume(sub, src_ref, first_key=None):
        start = pl.multiple_of(sub * tk, tk)
        for hh in range(heads_per_step):
            st = src_ref[hh]
            if first_key is not None:
                key = lax.broadcasted_iota(jnp.int32, (tk, tq), 0) + first_key
                qry = lax.broadcasted_iota(jnp.int32, (tk, tq), 1)
                st = jnp.where(key <= qry, st, NEG_INF)
            m_old = m_ref[hh]
            m_new = jnp.maximum(m_old, jnp.max(st, axis=0, keepdims=True))
            pt = jnp.exp2(st - m_new).astype(BF16)
            alpha = jnp.exp2(m_old - m_new)
            vt = vt_ref[0, hh, :, pl.ds(start, tk)]
            acc_ref[hh] = alpha * acc_ref[hh] + jnp.dot(vt, pt, preferred_element_type=F32)
            m_ref[hh] = m_new

    subs = tq // tk
    assert subs == 2
    scores(0, s0_ref)

    def trip(j, carry):
        scores(2 * j + 1, s1_ref)
        consume(2 * j, s0_ref)
        scores(2 * j + 2, s0_ref)
        consume(2 * j + 1, s1_ref)
        return carry

    lax.fori_loop(0, qi, trip, 0)
    scores(2 * qi + 1, s1_ref)
    consume(2 * qi, s0_ref, first_key=0)
    consume(2 * qi + 1, s1_ref, first_key=tk)
    outs = []
    for hh in range(heads_per_step):
        acc = acc_ref[hh].T
        outs.append(acc[:, :HEAD_DIM] / acc[:, HEAD_DIM:HEAD_DIM + 1])
    o_ref[0] = jnp.concatenate(outs, axis=-1).astype(o_ref.dtype)


def _attention(qt, k, vt, *, tq):
    b, nh, s, dk = k.shape
    hps = LANES // HEAD_DIM
    tk = tq // 2
    kern = functools.partial(_attn_kernel, tq=tq, tk=tk, heads_per_step=hps)
    return pl.pallas_call(
        kern,
        grid=(b, nh // hps, s // tq),
        in_specs=[
            pl.BlockSpec((1, hps, dk, tq), lambda bi, hi, qi: (bi, hi, 0, qi)),
            pl.BlockSpec((1, hps, s, dk), lambda bi, hi, qi: (bi, hi, 0, 0)),
            pl.BlockSpec((1, hps, dk, s), lambda bi, hi, qi: (bi, hi, 0, 0)),
        ],
        out_specs=pl.BlockSpec((1, tq, LANES), lambda bi, hi, qi: (bi, qi, hi)),
        out_shape=jax.ShapeDtypeStruct((b, s, nh * HEAD_DIM), BF16),
        scratch_shapes=[pltpu.VMEM((hps, 1, tq), F32), pltpu.VMEM((hps, dk, tq), F32),
                        pltpu.VMEM((hps, tk, tq), F32), pltpu.VMEM((hps, tk, tq), F32)],
        compiler_params=pltpu.CompilerParams(
            dimension_semantics=("arbitrary", "arbitrary", "arbitrary"), vmem_limit_bytes=VMEM_LIMIT),
        name="attention",
    )(qt, k, vt)


def _attn_out_kernel(o_ref, x_ref, g_ref, wo_ref, wr_ref, ltri_ref, x3_ref, ri_ref, rw_ref, cnt_ref,
                     carry_ref, *, tm):
    i = pl.program_id(0)
    g = g_ref[...]
    y = jnp.dot(o_ref[...], wo_ref[...], preferred_element_type=F32)
    x3 = x_ref[...] + _rms(y, g[0:1])
    x3_ref[...] = x3
    h = _rms(x3, g[1:2])
    logits = jnp.dot(h, wr_ref[...], preferred_element_type=F32, precision=lax.Precision.HIGHEST)

    lane = lax.broadcasted_iota(jnp.int32, (tm, LANES), 1)
    valid = lane < N_EXPERTS
    logits = jnp.where(valid, logits, -jnp.inf)
    m1 = jnp.max(logits, axis=-1, keepdims=True)
    e1 = jnp.min(jnp.where(logits == m1, lane, LANES), axis=-1, keepdims=True)
    rest = jnp.where(lane == e1, -jnp.inf, logits)
    m2 = jnp.max(rest, axis=-1, keepdims=True)
    e2 = jnp.min(jnp.where(rest == m2, lane, LANES), axis=-1, keepdims=True)
    t = jnp.exp(m2 - m1)
    w1 = 1.0 / (1.0 + t)
    w2 = t / (1.0 + t)

    @pl.when(i == 0)
    def _():
        carry_ref[...] = jnp.zeros_like(carry_ref)

    hot1 = lane == e1
    hot2 = lane == e2
    cnt = jnp.where(hot1 | hot2, 1.0, 0.0)
    before = jnp.dot(ltri_ref[...], cnt.astype(BF16), preferred_element_type=F32) + carry_ref[...]
    r1 = jnp.sum(jnp.where(hot1, before, 0.0), axis=-1, keepdims=True)
    r2 = jnp.sum(jnp.where(hot2, before, 0.0), axis=-1, keepdims=True)
    carry_ref[...] = carry_ref[...] + jnp.sum(cnt, axis=0, keepdims=True)
    cnt_ref[...] = jnp.broadcast_to(carry_ref[...], cnt_ref.shape).astype(jnp.int32)

    ri_ref[...] = jnp.where(lane == 0, e1, jnp.where(lane == 1, e2, jnp.where(
        lane == 2, r1.astype(jnp.int32), jnp.where(lane == 3, r2.astype(jnp.int32), 0))))
    rw_ref[...] = jnp.where(lane == 0, w1, jnp.where(lane == 1, w2, 0.0))


def _attn_out(o, x, g2, w_out, w_router_pad, *, tm):
    n, d = x.shape
    ltri = jnp.tril(jnp.ones((tm, tm), BF16), k=-1)
    kern = functools.partial(_attn_out_kernel, tm=tm)
    row_spec = pl.BlockSpec((tm, d), lambda i: (i, 0))
    meta_spec = pl.BlockSpec((tm, LANES), lambda i: (i, 0))
    return pl.pallas_call(
        kern,
        grid=(n // tm,),
        in_specs=[row_spec, row_spec, _resident((2, d)), _resident(w_out.shape),
                  _resident(w_router_pad.shape), _resident((tm, tm))],
        out_specs=[row_spec, meta_spec, meta_spec, pl.BlockSpec((8, LANES), lambda i: (0, 0))],
        out_shape=[jax.ShapeDtypeStruct((n, d), F32),
                   jax.ShapeDtypeStruct((n, LANES), jnp.int32),
                   jax.ShapeDtypeStruct((n, LANES), F32),
                   jax.ShapeDtypeStruct((8, LANES), jnp.int32)],
        scratch_shapes=[pltpu.VMEM((1, LANES), F32)],
        compiler_params=pltpu.CompilerParams(
            dimension_semantics=("arbitrary",), vmem_limit_bytes=VMEM_LIMIT),
        name="attn_out",
    )(o, x, g2, w_out, w_router_pad, ltri)


def _row_copy(src, dst, sem):
    return pltpu.make_async_copy(src, dst, sem)


def _dispatch_kernel(fill_ref, pos_ref, x_ref, g_ref, xs_ref, h_buf, z_buf, sem, *, tb):
    i = pl.program_id(0)
    h_buf[...] = _rms(x_ref[...], g_ref[...])

    def issue(t, _):
        for slot in range(2):
            p = pos_ref[0, slot, t]
            _row_copy(h_buf.at[pl.ds(t, 1), :], xs_ref.at[pl.ds(p, 1), :], sem).start()
        return 0

    lax.fori_loop(0, tb, issue, 0, unroll=ISSUE_UNROLL)

    @pl.when(i == 0)
    def _():
        z_buf[...] = jnp.zeros_like(z_buf)
        for e in range(N_EXPERTS):
            lo = fill_ref[0, e]
            hi = fill_ref[1, e]

            def fill(r, _):
                _row_copy(z_buf, xs_ref.at[pl.ds(r, 1), :], sem).start()
                return 0

            lax.fori_loop(lo, hi, fill, 0)

            def drain(r, _):
                _row_copy(z_buf, xs_ref.at[pl.ds(0, 1), :], sem).wait()
                return 0

            lax.fori_loop(lo, hi, drain, 0)

    for _ in range(2 * tb):
        _row_copy(h_buf.at[pl.ds(0, 1), :], xs_ref.at[pl.ds(0, 1), :], sem).wait()


def _dispatch(x3, g, pos, fill, n_rows, *, tb):
    n, d = x3.shape
    kern = functools.partial(_dispatch_kernel, tb=tb)
    return pl.pallas_call(
        kern,
        grid_spec=pltpu.PrefetchScalarGridSpec(
            num_scalar_prefetch=1,
            grid=(n // tb,),
            in_specs=[
                pl.BlockSpec((1, 2, tb), lambda i, f: (i, 0, 0), memory_space=pltpu.SMEM),
                pl.BlockSpec((tb, d), lambda i, f: (i, 0)),
                pl.BlockSpec((1, d), lambda i, f: (0, 0)),
            ],
            out_specs=pl.BlockSpec(memory_space=pl.ANY),
            scratch_shapes=[pltpu.VMEM((tb, d), F32), pltpu.VMEM((1, d), F32),
                            pltpu.SemaphoreType.DMA(())],
        ),
        out_shape=jax.ShapeDtypeStruct((n_rows, d), F32),
        compiler_params=pltpu.CompilerParams(
            dimension_semantics=("arbitrary",), vmem_limit_bytes=VMEM_LIMIT),
        name="dispatch",
    )(fill, pos, x3, g)


def _experts_kernel(te_ref, nt_ref, xs_ref, wg_ref, wu_ref, wd_ref, ys_ref, xb_ref, acc_ref):
    i = pl.program_id(0)
    j = pl.program_id(1)
    nj = pl.num_programs(1)

    @pl.when(i < nt_ref[0])
    def _():
        @pl.when(j == 0)
        def _():
            xb_ref[...] = xs_ref[...].astype(BF16)
            acc_ref[...] = jnp.zeros_like(acc_ref)

        xb = xb_ref[...]
        gate = jnp.dot(xb, wg_ref[0], preferred_element_type=F32)
        up = jnp.dot(xb, wu_ref[0], preferred_element_type=F32)
        act = (gate * jax.nn.sigmoid(gate) * up).astype(BF16)
        acc_ref[...] += jnp.dot(act, wd_ref[0], preferred_element_type=F32)

        @pl.when(j == nj - 1)
        def _():
            ys_ref[...] = acc_ref[...]

    @pl.when((i >= nt_ref[0]) & (j == nj - 1))
    def _():
        ys_ref[...] = jnp.zeros_like(ys_ref)


def _experts(xs, tile_expert, n_tiles, w_gate, w_up, w_down, *, tm, tf):
    n_rows, d = xs.shape
    dff = w_gate.shape[-1]
    max_tiles = n_rows // tm

    def row_map(i, j, te, nt):
        return (jnp.minimum(i, nt[0] - 1), 0)

    def wcol_map(i, j, te, nt):
        return (te[i], 0, j)

    def wrow_map(i, j, te, nt):
        return (te[i], j, 0)

    return pl.pallas_call(
        _experts_kernel,
        grid_spec=pltpu.PrefetchScalarGridSpec(
            num_scalar_prefetch=2,
            grid=(max_tiles, dff // tf),
            in_specs=[
                pl.BlockSpec((tm, d), row_map),
                pl.BlockSpec((1, d, tf), wcol_map),
                pl.BlockSpec((1, d, tf), wcol_map),
                pl.BlockSpec((1, tf, d), wrow_map),
            ],
            out_specs=pl.BlockSpec((tm, d), lambda i, j, te, nt: (i, 0)),
            scratch_shapes=[pltpu.VMEM((tm, d), BF16), pltpu.VMEM((tm, d), F32)],
        ),
        out_shape=jax.ShapeDtypeStruct((n_rows, d), F32),
        compiler_params=pltpu.CompilerParams(
            dimension_semantics=("arbitrary", "arbitrary"), vmem_limit_bytes=VMEM_LIMIT),
        name="experts",
    )(tile_expert, n_tiles, xs, w_gate, w_up, w_down)


def _combine_kernel(pos_ref, ys_ref, x_ref, rw_ref, g_ref, o_ref, a_buf, b_buf, sem, *, tb):
    def issue(t, _):
        _row_copy(ys_ref.at[pl.ds(pos_ref[0, 0, t], 1), :], a_buf.at[pl.ds(t, 1), :], sem).start()
        _row_copy(ys_ref.at[pl.ds(pos_ref[0, 1, t], 1), :], b_buf.at[pl.ds(t, 1), :], sem).start()
        return 0

    lax.fori_loop(0, tb, issue, 0, unroll=ISSUE_UNROLL)

    for _ in range(2 * tb):
        _row_copy(ys_ref.at[pl.ds(0, 1), :], a_buf.at[pl.ds(0, 1), :], sem).wait()
    rw = rw_ref[...]
    y = rw[:, 0:1] * a_buf[...] + rw[:, 1:2] * b_buf[...]
    o_ref[...] = x_ref[...] + _rms(y, g_ref[...])


def _combine(ys, x3, rw, g, pos, *, tb):
    n, d = x3.shape
    kern = functools.partial(_combine_kernel, tb=tb)
    return pl.pallas_call(
        kern,
        grid=(n // tb,),
        in_specs=[
            pl.BlockSpec((1, 2, tb), lambda i: (i, 0, 0), memory_space=pltpu.SMEM),
            pl.BlockSpec(memory_space=pl.ANY),
            pl.BlockSpec((tb, d), lambda i: (i, 0)),
            pl.BlockSpec((tb, LANES), lambda i: (i, 0)),
            pl.BlockSpec((1, d), lambda i: (0, 0)),
        ],
        out_specs=pl.BlockSpec((tb, d), lambda i: (i, 0)),
        out_shape=jax.ShapeDtypeStruct((n, d), F32),
        scratch_shapes=[pltpu.VMEM((tb, d), F32), pltpu.VMEM((tb, d), F32),
                        pltpu.SemaphoreType.DMA(())],
        compiler_params=pltpu.CompilerParams(
            dimension_semantics=("arbitrary",), vmem_limit_bytes=VMEM_LIMIT),
        name="combine",
    )(pos, ys, x3, rw, g)


def _pick_tile(n, pref):
    t = min(pref, n)
    while n % t:
        t //= 2
    return t


def kernel(x, norm_g, pool_w, pool_scale, attn_w_in, attn_b_f, attn_w_out, ffn_w_gate, ffn_w_up, ffn_w_down,
           moe_w_router, moe_w_gate, moe_w_up, moe_w_down):
    b, s, d = x.shape
    n = b * s
    nh = d // HEAD_DIM
    tm = _pick_tile(s, 512)

    x = _layer0(x, norm_g[0], pool_w[0].astype(BF16), pool_scale[0],
                ffn_w_gate[0].astype(BF16), ffn_w_up[0].astype(BF16), ffn_w_down[0].astype(BF16), tm=tm)

    w_in = attn_w_in[0]
    w_f = jnp.zeros((d, LANES), F32).at[:, :nh].set(w_in[:, 3 * d:]).astype(BF16)
    b_f = jnp.zeros((1, LANES), F32).at[0, :nh].set(attn_b_f[0])
    w_qvt = jnp.concatenate([w_in[:, :d], w_in[:, 2 * d:3 * d]], axis=1).T.astype(BF16)
    qt, k, vt = _qkv(x, norm_g[1, 0:1], w_in[:, d:2 * d].astype(BF16), w_qvt, w_f, b_f, tm=tm)
    o = _attention(qt, k, vt, tq=tm)

    w_router = jnp.zeros((d, LANES), F32).at[:, :N_EXPERTS].set(moe_w_router[0])
    x3, ri, rw, counts = _attn_out(o.reshape(n, d), x.reshape(n, d), norm_g[1, 1:3],
                                   attn_w_out[0].astype(BF16), w_router, tm=tm)

    tme = _pick_tile(n, 512)
    counts = counts[0, :N_EXPERTS]
    padded = (counts + tme - 1) // tme * tme
    ends = jnp.cumsum(padded)
    starts = ends - padded
    n_rows = 2 * n + N_EXPERTS * tme
    pos = jnp.stack([starts[ri[:, 0]] + ri[:, 2], starts[ri[:, 1]] + ri[:, 3]], axis=0)
    tb = _pick_tile(n, 256)
    pos = pos.reshape(2, n // tb, tb).transpose(1, 0, 2)
    fill_hi = ends.at[N_EXPERTS - 1].set(n_rows)
    fill = jnp.stack([starts + counts, fill_hi], axis=0).astype(jnp.int32)
    tile_start = jnp.arange(n_rows // tme, dtype=jnp.int32) * tme
    tile_expert = jnp.minimum(jnp.sum(tile_start[:, None] >= ends[None, :], axis=1), N_EXPERTS - 1).astype(jnp.int32)
    n_tiles = (ends[-1:] // tme).astype(jnp.int32)

    xs = _dispatch(x3, norm_g[1, 2:3], pos, fill, n_rows, tb=tb)
    dffe = moe_w_gate.shape[-1]
    tf = dffe // 2 if (dffe // 2) % LANES == 0 else dffe
    ys = _experts(xs, tile_expert, n_tiles, moe_w_gate[0].astype(BF16), moe_w_up[0].astype(BF16),
                  moe_w_down[0].astype(BF16), tm=tme, tf=tf)
    out = _combine(ys, x3, rw, norm_g[1, 3:4], pos, tb=tb)
    return out.reshape(b, s, d)
```

```python
import functools

import jax
import jax.numpy as jnp
from jax import lax
from jax.experimental import pallas as pl
from jax.experimental.pallas import tpu as pltpu

F32 = jnp.float32
BF16 = jnp.bfloat16

RMS_EPS = 1e-6
HEAD_DIM = 64
POOL_WINDOWS = (2, 4, 8, 16)
POOL_HALO = 16
N_EXPERTS = 8
NEG_INF = -1e30
LANES = 128
VMEM_LIMIT = 56 * 1024 * 1024
ISSUE_UNROLL = 8
LOG2E = 1.4426950408889634


def _rms(x, g):
    ms = jnp.mean(x * x, axis=-1, keepdims=True)
    return x * lax.rsqrt(ms + RMS_EPS) * g


def _split3(c):
    hi = c.astype(BF16).astype(F32)
    r = c - hi
    mid = r.astype(BF16).astype(F32)
    lo = r - mid
    return hi, mid, lo


def _resident(shape):
    nd = len(shape)
    return pl.BlockSpec(shape, lambda *_: (0,) * nd, pipeline_mode=pl.Buffered(1))


def _layer0_kernel(x_ref, xp_ref, g_ref, pw_ref, ps_ref, wg_ref, wu_ref, wd_ref, o_ref,
                   buf_a, buf_b, *, tm, ff_chunk):
    s = pl.program_id(1)
    d = x_ref.shape[-1]
    gd = d // len(POOL_WINDOWS)
    x = x_ref[0]
    g = g_ref[...]
    h = _rms(x, g[0:1])
    hp = _rms(xp_ref[0], g[0:1])
    hp = jnp.where(s > 0, hp, 0.0)

    lo = 8
    top = tm + POOL_HALO + lo
    zeros8 = jnp.zeros((lo, d), F32)
    buf_a[0:lo, :] = zeros8
    buf_b[0:lo, :] = zeros8
    buf_a[lo:lo + POOL_HALO, :] = hp
    buf_a[lo + POOL_HALO:top, :] = h
    body = lo + POOL_HALO

    def shifted_sum(src, k, c0, rows0):
        n = top - rows0
        return src[rows0:top, c0:] + src[rows0 - k:rows0 - k + n, c0:]

    sums = []
    sums.append((buf_a[body:top, 0:gd] + buf_a[body - 1:top - 1, 0:gd]))
    buf_b[lo:top, gd:] = shifted_sum(buf_a, 1, gd, lo)
    sums.append(buf_b[body:top, gd:2 * gd] + buf_b[body - 2:top - 2, gd:2 * gd])
    buf_a[lo:top, 2 * gd:] = shifted_sum(buf_b, 2, 2 * gd, lo)
    sums.append(buf_a[body:top, 2 * gd:3 * gd] + buf_a[body - 4:top - 4, 2 * gd:3 * gd])
    buf_b[lo:top, 3 * gd:] = shifted_sum(buf_a, 4, 3 * gd, lo)
    sums.append(buf_b[body:top, 3 * gd:] + buf_b[body - 8:top - 8, 3 * gd:])

    pos = s * tm + lax.broadcasted_iota(jnp.int32, (tm, 1), 0)
    mixed = []
    for gi, w in enumerate(POOL_WINDOWS):
        count = jnp.minimum(pos + 1, w).astype(F32)
        pooled = sums[gi] / count - h[:, gi * gd:(gi + 1) * gd]
        mixed.append(jnp.dot(pooled.astype(BF16), pw_ref[gi], preferred_element_type=F32))
    y = jnp.concatenate(mixed, axis=-1) * ps_ref[...]
    x1 = x + _rms(y, g[1:2])

    h2 = _rms(x1, g[2:3]).astype(BF16)
    dff = wg_ref.shape[1]
    acc = jnp.zeros((tm, d), F32)
    for c0 in range(0, dff, ff_chunk):
        gate = jnp.dot(h2, wg_ref[:, c0:c0 + ff_chunk], preferred_element_type=F32)
        up = jnp.dot(h2, wu_ref[:, c0:c0 + ff_chunk], preferred_element_type=F32)
        act = (gate * jax.nn.sigmoid(gate) * up).astype(BF16)
        acc = acc + jnp.dot(act, wd_ref[c0:c0 + ff_chunk, :], preferred_element_type=F32)
    o_ref[0] = x1 + _rms(acc, g[3:4])


def _layer0(x, g4, pool_w, pool_scale, w_gate, w_up, w_down, *, tm):
    b, s, d = x.shape
    dff = w_gate.shape[1]
    ff_chunk = dff // 2 if (dff // 2) % LANES == 0 else dff
    halo_per_tile = tm // POOL_HALO
    kern = functools.partial(_layer0_kernel, tm=tm, ff_chunk=ff_chunk)
    return pl.pallas_call(
        kern,
        grid=(b, s // tm),
        in_specs=[
            pl.BlockSpec((1, tm, d), lambda bi, si: (bi, si, 0)),
            pl.BlockSpec((1, POOL_HALO, d), lambda bi, si: (bi, jnp.maximum(si * halo_per_tile - 1, 0), 0)),
            _resident((4, d)),
            _resident(pool_w.shape),
            _resident((1, d)),
            _resident(w_gate.shape),
            _resident(w_up.shape),
            _resident(w_down.shape),
        ],
        out_specs=pl.BlockSpec((1, tm, d), lambda bi, si: (bi, si, 0)),
        out_shape=jax.ShapeDtypeStruct((b, s, d), F32),
        scratch_shapes=[pltpu.VMEM((tm + POOL_HALO + 8, d), F32),
                        pltpu.VMEM((tm + POOL_HALO + 8, d), F32)],
        compiler_params=pltpu.CompilerParams(
            dimension_semantics=("arbitrary", "arbitrary"), vmem_limit_bytes=VMEM_LIMIT),
        name="layer0",
    )(x, x, g4, pool_w, pool_scale.reshape(1, d), w_gate, w_up, w_down)


def _qkv_kernel(x_ref, g_ref, wk_ref, wqvt_ref, wf_ref, bf_ref, tri_ref, hsel_ref,
                qt_ref, k_ref, vt_ref, c2_ref, qn_ref, kn_ref, carry_ref, *, tm, n_heads):
    s = pl.program_id(1)
    d = x_ref.shape[-1]
    h = _rms(x_ref[0], g_ref[...]).astype(BF16)
    nt = (((1,), (1,)), ((), ()))
    kproj = jnp.dot(h, wk_ref[...], preferred_element_type=F32)
    qvt = lax.dot_general(wqvt_ref[...], h, nt, preferred_element_type=F32)
    z = jnp.dot(h, wf_ref[...], preferred_element_type=F32) + bf_ref[...]
    log_f = jnp.minimum(z, 0.0) - jnp.log(1.0 + jnp.exp(-jnp.abs(z)))

    @pl.when(s == 0)
    def _():
        carry_ref[...] = jnp.zeros_like(carry_ref)

    tri = tri_ref[...]
    c = carry_ref[...]
    for piece in _split3(log_f):
        c = c + jnp.dot(tri, piece.astype(BF16), preferred_element_type=F32)
    carry_ref[...] = c[tm - 1:tm, :]
    c = c * LOG2E
    c2_ref[0] = c
    c_hi, c_mid, c_lo = _split3(c)
    ct_hi, ct_mid, ct_lo = _split3(c.T)

    lane = lax.broadcasted_iota(jnp.int32, (tm, HEAD_DIM), 1)
    row = lax.broadcasted_iota(jnp.int32, (HEAD_DIM, tm), 0)
    aug_vt = jnp.where(row == 0, 1.0, 0.0)
    scale = HEAD_DIM ** -0.5 * LOG2E
    q_sq = jnp.square(qvt[:d, :] * scale).reshape(n_heads, HEAD_DIM, tm)
    qn_ref[0, 0] = jnp.broadcast_to(jnp.max(jnp.sum(q_sq, axis=1), axis=1, keepdims=True), qn_ref.shape[2:])
    k_sq = jnp.dot(jnp.square(kproj).astype(BF16), hsel_ref[...], preferred_element_type=F32)
    kn_ref[0, 0] = jnp.broadcast_to(jnp.max(k_sq, axis=0, keepdims=True), kn_ref.shape[2:])
    for hd in range(n_heads):
        aug_k = jnp.where(lane == 3, -c_hi[:, hd:hd + 1], jnp.where(
            lane == 4, -c_mid[:, hd:hd + 1], jnp.where(
                lane == 5, -c_lo[:, hd:hd + 1], jnp.where(lane < 3, 1.0, 0.0))))
        aug_qt = jnp.where(row == 0, ct_hi[hd:hd + 1, :], jnp.where(
            row == 1, ct_mid[hd:hd + 1, :], jnp.where(
                row == 2, ct_lo[hd:hd + 1, :], jnp.where(row < 6, 1.0, 0.0))))
        c0 = hd * HEAD_DIM
        k_ref[0, hd] = jnp.concatenate([kproj[:, c0:c0 + HEAD_DIM], aug_k], axis=-1).astype(BF16)
        qt_ref[0, hd] = jnp.concatenate([qvt[c0:c0 + HEAD_DIM, :] * scale, aug_qt], axis=0).astype(BF16)
        vt_ref[0, hd] = jnp.concatenate([qvt[d + c0:d + c0 + HEAD_DIM, :], aug_vt], axis=0).astype(BF16)


def _qkv(x, g, w_k, w_qvt, w_f, b_f, *, tm):
    b, s, d = x.shape
    n_heads = d // HEAD_DIM
    dk = 2 * HEAD_DIM
    tri = jnp.tril(jnp.ones((tm, tm), BF16))
    hsel = (jnp.arange(d)[:, None] // HEAD_DIM == jnp.arange(LANES)[None, :]).astype(BF16)
    kern = functools.partial(_qkv_kernel, tm=tm, n_heads=n_heads)
    row_sds = jax.ShapeDtypeStruct((b, n_heads, s, dk), BF16)
    col_sds = jax.ShapeDtypeStruct((b, n_heads, dk, s), BF16)
    row_spec = pl.BlockSpec((1, n_heads, tm, dk), lambda bi, si: (bi, 0, si, 0))
    col_spec = pl.BlockSpec((1, n_heads, dk, tm), lambda bi, si: (bi, 0, 0, si))
    return pl.pallas_call(
        kern,
        grid=(b, s // tm),
        in_specs=[
            pl.BlockSpec((1, tm, d), lambda bi, si: (bi, si, 0)),
            _resident((1, d)),
            _resident(w_k.shape),
            _resident(w_qvt.shape),
            _resident(w_f.shape),
            _resident(b_f.shape),
            _resident((tm, tm)),
            _resident((d, LANES)),
        ],
        out_specs=[col_spec, row_spec, col_spec,
                   pl.BlockSpec((1, tm, LANES), lambda bi, si: (bi, si, 0)),
                   pl.BlockSpec((1, 1, n_heads, LANES), lambda bi, si: (bi, si, 0, 0)),
                   pl.BlockSpec((1, 1, 8, LANES), lambda bi, si: (bi, si, 0, 0))],
        out_shape=[col_sds, row_sds, col_sds,
                   jax.ShapeDtypeStruct((b, s, LANES), F32),
                   jax.ShapeDtypeStruct((b, s // tm, n_heads, LANES), F32),
                   jax.ShapeDtypeStruct((b, s // tm, 8, LANES), F32)],
        scratch_shapes=[pltpu.VMEM((1, LANES), F32)],
        compiler_params=pltpu.CompilerParams(
            dimension_semantics=("arbitrary", "arbitrary"), vmem_limit_bytes=VMEM_LIMIT),
        name="qkv",
    )(x, g, w_k, w_qvt, w_f, b_f, tri, hsel)


def _attn_kernel(first_ref, qt_ref, k_ref, vt_ref, o_ref, m_ref, acc_ref, s0_ref, s1_ref, *, tq, tk, heads_per_step):
    qi = pl.program_id(2)
    step = (pl.program_id(0) * pl.num_programs(1) + pl.program_id(1)) * pl.num_programs(2) + qi
    first = first_ref[step]
    m_ref[...] = jnp.full(m_ref.shape, NEG_INF, F32)
    acc_ref[...] = jnp.zeros(acc_ref.shape, F32)

    def scores(sub, dst_ref):
        start = pl.multiple_of(sub * tk, tk)
        for hh in range(heads_per_step):
            k = k_ref[0, hh, pl.ds(start, tk), :]
            dst_ref[hh] = jnp.dot(k, qt_ref[0, hh], preferred_element_type=F32)

    def consume(sub, src_ref, first_key=None):
        start = pl.multiple_of(sub * tk, tk)
        for hh in range(heads_per_step):
            st = src_ref[hh]
            if first_key is not None:
                key = lax.broadcasted_iota(jnp.int32, (tk, tq), 0) + first_key
                qry = lax.broadcasted_iota(jnp.int32, (tk, tq), 1)
                st = jnp.where(key <= qry, st, NEG_INF)
            m_old = m_ref[hh]
            m_new = jnp.maximum(m_old, jnp.max(st, axis=0, keepdims=True))
            pt = jnp.exp2(st - m_new).astype(BF16)
            alpha = jnp.exp2(m_old - m_new)
            vt = vt_ref[0, hh, :, pl.ds(start, tk)]
            acc_ref[hh] = alpha * acc_ref[hh] + jnp.dot(vt, pt, preferred_element_type=F32)
            m_ref[hh] = m_new

    subs = tq // tk
    assert subs == 2
    scores(2 * first, s0_ref)

    def trip(j, carry):
        scores(2 * j + 1, s1_ref)
        consume(2 * j, s0_ref)
        scores(2 * j + 2, s0_ref)
        consume(2 * j + 1, s1_ref)
        return carry

    lax.fori_loop(first, qi, trip, 0)
    scores(2 * qi + 1, s1_ref)
    consume(2 * qi, s0_ref, first_key=0)
    consume(2 * qi + 1, s1_ref, first_key=tk)
    outs = []
    for hh in range(heads_per_step):
        acc = acc_ref[hh].T
        outs.append(acc[:, :HEAD_DIM] / acc[:, HEAD_DIM:HEAD_DIM + 1])
    o_ref[0] = jnp.concatenate(outs, axis=-1).astype(o_ref.dtype)


UNDERFLOW_LOG2 = 126.0
SKIP_MARGIN_LOG2 = 8.0


def _first_live_block(c2, qn, kn, *, tq, heads_per_step):
    b, s, _ = c2.shape
    nh = qn.shape[2]
    nq = s // tq
    qk = jnp.sqrt(jnp.max(qn[:, :, :, 0], axis=1) * jnp.max(kn[:, :, 0, :nh], axis=1)) * 1.03
    gap = 2.0 * qk + UNDERFLOW_LOG2 + SKIP_MARGIN_LOG2
    c_query = c2[:, 0::tq, :nh]
    c_key = c2[:, tq - 1::tq, :nh]
    dead = (c_key[:, None, :, :] - c_query[:, :, None, :]) > gap[:, None, None, :]
    dead = dead & (jnp.arange(nq)[None, None, :, None] < jnp.arange(nq)[None, :, None, None])
    first = jnp.sum(dead, axis=2).astype(jnp.int32)
    first = jnp.min(first.reshape(b, nq, nh // heads_per_step, heads_per_step), axis=-1)
    return first.transpose(0, 2, 1).reshape(-1)


def _attention(qt, k, vt, first, *, tq):
    b, nh, s, dk = k.shape
    hps = LANES // HEAD_DIM
    tk = tq // 2
    kern = functools.partial(_attn_kernel, tq=tq, tk=tk, heads_per_step=hps)
    return pl.pallas_call(
        kern,
        grid_spec=pltpu.PrefetchScalarGridSpec(
            num_scalar_prefetch=1,
            grid=(b, nh // hps, s // tq),
            in_specs=[
                pl.BlockSpec((1, hps, dk, tq), lambda bi, hi, qi, f: (bi, hi, 0, qi)),
                pl.BlockSpec((1, hps, s, dk), lambda bi, hi, qi, f: (bi, hi, 0, 0)),
                pl.BlockSpec((1, hps, dk, s), lambda bi, hi, qi, f: (bi, hi, 0, 0)),
            ],
            out_specs=pl.BlockSpec((1, tq, LANES), lambda bi, hi, qi, f: (bi, qi, hi)),
            scratch_shapes=[pltpu.VMEM((hps, 1, tq), F32), pltpu.VMEM((hps, dk, tq), F32),
                            pltpu.VMEM((hps, tk, tq), F32), pltpu.VMEM((hps, tk, tq), F32)],
        ),
        out_shape=jax.ShapeDtypeStruct((b, s, nh * HEAD_DIM), BF16),
        compiler_params=pltpu.CompilerParams(
            dimension_semantics=("arbitrary", "arbitrary", "arbitrary"), vmem_limit_bytes=VMEM_LIMIT),
        name="attention",
    )(first, qt, k, vt)


def _attn_out_kernel(o_ref, x_ref, g_ref, wo_ref, wr_ref, ltri_ref, x3_ref, ri_ref, rw_ref, cnt_ref,
                     carry_ref, *, tm):
    i = pl.program_id(0)
    g = g_ref[...]
    y = jnp.dot(o_ref[...], wo_ref[...], preferred_element_type=F32)
    x3 = x_ref[...] + _rms(y, g[0:1])
    x3_ref[...] = x3
    h = _rms(x3, g[1:2])
    logits = jnp.dot(h, wr_ref[...], preferred_element_type=F32, precision=lax.Precision.HIGHEST)

    lane = lax.broadcasted_iota(jnp.int32, (tm, LANES), 1)
    valid = lane < N_EXPERTS
    logits = jnp.where(valid, logits, -jnp.inf)
    m1 = jnp.max(logits, axis=-1, keepdims=True)
    e1 = jnp.min(jnp.where(logits == m1, lane, LANES), axis=-1, keepdims=True)
    rest = jnp.where(lane == e1, -jnp.inf, logits)
    m2 = jnp.max(rest, axis=-1, keepdims=True)
    e2 = jnp.min(jnp.where(rest == m2, lane, LANES), axis=-1, keepdims=True)
    t = jnp.exp(m2 - m1)
    w1 = 1.0 / (1.0 + t)
    w2 = t / (1.0 + t)

    @pl.when(i == 0)
    def _():
        carry_ref[...] = jnp.zeros_like(carry_ref)

    hot1 = lane == e1
    hot2 = lane == e2
    cnt = jnp.where(hot1 | hot2, 1.0, 0.0)
    before = jnp.dot(ltri_ref[...], cnt.astype(BF16), preferred_element_type=F32) + carry_ref[...]
    r1 = jnp.sum(jnp.where(hot1, before, 0.0), axis=-1, keepdims=True)
    r2 = jnp.sum(jnp.where(hot2, before, 0.0), axis=-1, keepdims=True)
    carry_ref[...] = carry_ref[...] + jnp.sum(cnt, axis=0, keepdims=True)
    cnt_ref[...] = jnp.broadcast_to(carry_ref[...], cnt_ref.shape).astype(jnp.int32)

    ri_ref[...] = jnp.where(lane == 0, e1, jnp.where(lane == 1, e2, jnp.where(
        lane == 2, r1.astype(jnp.int32), jnp.where(lane == 3, r2.astype(jnp.int32), 0))))
    rw_ref[...] = jnp.where(lane == 0, w1, jnp.where(lane == 1, w2, 0.0))


def _attn_out(o, x, g2, w_out, w_router_pad, *, tm):
    n, d = x.shape
    ltri = jnp.tril(jnp.ones((tm, tm), BF16), k=-1)
    kern = functools.partial(_attn_out_kernel, tm=tm)
    row_spec = pl.BlockSpec((tm, d), lambda i: (i, 0))
    meta_spec = pl.BlockSpec((tm, LANES), lambda i: (i, 0))
    return pl.pallas_call(
        kern,
        grid=(n // tm,),
        in_specs=[row_spec, row_spec, _resident((2, d)), _resident(w_out.shape),
                  _resident(w_router_pad.shape), _resident((tm, tm))],
        out_specs=[row_spec, meta_spec, meta_spec, pl.BlockSpec((8, LANES), lambda i: (0, 0))],
        out_shape=[jax.ShapeDtypeStruct((n, d), F32),
                   jax.ShapeDtypeStruct((n, LANES), jnp.int32),
                   jax.ShapeDtypeStruct((n, LANES), F32),
                   jax.ShapeDtypeStruct((8, LANES), jnp.int32)],
        scratch_shapes=[pltpu.VMEM((1, LANES), F32)],
        compiler_params=pltpu.CompilerParams(
            dimension_semantics=("arbitrary",), vmem_limit_bytes=VMEM_LIMIT),
        name="attn_out",
    )(o, x, g2, w_out, w_router_pad, ltri)


def _row_copy(src, dst, sem):
    return pltpu.make_async_copy(src, dst, sem)


def _dispatch_kernel(fill_ref, pos_ref, x_ref, g_ref, xs_ref, h_buf, z_buf, sem, *, tb):
    i = pl.program_id(0)
    h_buf[...] = _rms(x_ref[...], g_ref[...])

    def issue(t, _):
        for slot in range(2):
            p = pos_ref[0, slot, t]
            _row_copy(h_buf.at[pl.ds(t, 1), :], xs_ref.at[pl.ds(p, 1), :], sem).start()
        return 0

    lax.fori_loop(0, tb, issue, 0, unroll=ISSUE_UNROLL)

    @pl.when(i == 0)
    def _():
        z_buf[...] = jnp.zeros_like(z_buf)
        for e in range(N_EXPERTS):
            lo = fill_ref[0, e]
            hi = fill_ref[1, e]

            def fill(r, _):
                _row_copy(z_buf, xs_ref.at[pl.ds(r, 1), :], sem).start()
                return 0

            lax.fori_loop(lo, hi, fill, 0)

            def drain(r, _):
                _row_copy(z_buf, xs_ref.at[pl.ds(0, 1), :], sem).wait()
                return 0

            lax.fori_loop(lo, hi, drain, 0)

    for _ in range(2 * tb):
        _row_copy(h_buf.at[pl.ds(0, 1), :], xs_ref.at[pl.ds(0, 1), :], sem).wait()


def _dispatch(x3, g, pos, fill, n_rows, *, tb):
    n, d = x3.shape
    kern = functools.partial(_dispatch_kernel, tb=tb)
    return pl.pallas_call(
        kern,
        grid_spec=pltpu.PrefetchScalarGridSpec(
            num_scalar_prefetch=1,
            grid=(n // tb,),
            in_specs=[
                pl.BlockSpec((1, 2, tb), lambda i, f: (i, 0, 0), memory_space=pltpu.SMEM),
                pl.BlockSpec((tb, d), lambda i, f: (i, 0)),
                pl.BlockSpec((1, d), lambda i, f: (0, 0)),
            ],
            out_specs=pl.BlockSpec(memory_space=pl.ANY),
            scratch_shapes=[pltpu.VMEM((tb, d), F32), pltpu.VMEM((1, d), F32),
                            pltpu.SemaphoreType.DMA(())],
        ),
        out_shape=jax.ShapeDtypeStruct((n_rows, d), F32),
        compiler_params=pltpu.CompilerParams(
            dimension_semantics=("arbitrary",), vmem_limit_bytes=VMEM_LIMIT),
        name="dispatch",
    )(fill, pos, x3, g)


def _experts_kernel(te_ref, nt_ref, xs_ref, wg_ref, wu_ref, wd_ref, ys_ref, xb_ref, acc_ref):
    i = pl.program_id(0)
    j = pl.program_id(1)
    nj = pl.num_programs(1)

    @pl.when(i < nt_ref[0])
    def _():
        @pl.when(j == 0)
        def _():
            xb_ref[...] = xs_ref[...].astype(BF16)
            acc_ref[...] = jnp.zeros_like(acc_ref)

        xb = xb_ref[...]
        gate = jnp.dot(xb, wg_ref[0], preferred_element_type=F32)
        up = jnp.dot(xb, wu_ref[0], preferred_element_type=F32)
        act = (gate * jax.nn.sigmoid(gate) * up).astype(BF16)
        acc_ref[...] += jnp.dot(act, wd_ref[0], preferred_element_type=F32)

        @pl.when(j == nj - 1)
        def _():
            ys_ref[...] = acc_ref[...]

    @pl.when((i >= nt_ref[0]) & (j == nj - 1))
    def _():
        ys_ref[...] = jnp.zeros_like(ys_ref)


def _experts(xs, tile_expert, n_tiles, w_gate, w_up, w_down, *, tm, tf):
    n_rows, d = xs.shape
    dff = w_gate.shape[-1]
    max_tiles = n_rows // tm

    def row_map(i, j, te, nt):
        return (jnp.minimum(i, nt[0] - 1), 0)

    def wcol_map(i, j, te, nt):
        return (te[i], 0, j)

    def wrow_map(i, j, te, nt):
        return (te[i], j, 0)

    return pl.pallas_call(
        _experts_kernel,
        grid_spec=pltpu.PrefetchScalarGridSpec(
            num_scalar_prefetch=2,
            grid=(max_tiles, dff // tf),
            in_specs=[
                pl.BlockSpec((tm, d), row_map),
                pl.BlockSpec((1, d, tf), wcol_map),
                pl.BlockSpec((1, d, tf), wcol_map),
                pl.BlockSpec((1, tf, d), wrow_map),
            ],
            out_specs=pl.BlockSpec((tm, d), lambda i, j, te, nt: (i, 0)),
            scratch_shapes=[pltpu.VMEM((tm, d), BF16), pltpu.VMEM((tm, d), F32)],
        ),
        out_shape=jax.ShapeDtypeStruct((n_rows, d), F32),
        compiler_params=pltpu.CompilerParams(
            dimension_semantics=("arbitrary", "arbitrary"), vmem_limit_bytes=VMEM_LIMIT),
        name="experts",
    )(tile_expert, n_tiles, xs, w_gate, w_up, w_down)


def _combine_kernel(pos_ref, ys_ref, x_ref, rw_ref, g_ref, o_ref, a_buf, b_buf, sem, *, tb):
    def issue(t, _):
        _row_copy(ys_ref.at[pl.ds(pos_ref[0, 0, t], 1), :], a_buf.at[pl.ds(t, 1), :], sem).start()
        _row_copy(ys_ref.at[pl.ds(pos_ref[0, 1, t], 1), :], b_buf.at[pl.ds(t, 1), :], sem).start()
        return 0

    lax.fori_loop(0, tb, issue, 0, unroll=ISSUE_UNROLL)

    for _ in range(2 * tb):
        _row_copy(ys_ref.at[pl.ds(0, 1), :], a_buf.at[pl.ds(0, 1), :], sem).wait()
    rw = rw_ref[...]
    y = rw[:, 0:1] * a_buf[...] + rw[:, 1:2] * b_buf[...]
    o_ref[...] = x_ref[...] + _rms(y, g_ref[...])


def _combine(ys, x3, rw, g, pos, *, tb):
    n, d = x3.shape
    kern = functools.partial(_combine_kernel, tb=tb)
    return pl.pallas_call(
        kern,
        grid=(n // tb,),
        in_specs=[
            pl.BlockSpec((1, 2, tb), lambda i: (i, 0, 0), memory_space=pltpu.SMEM),
            pl.BlockSpec(memory_space=pl.ANY),
            pl.BlockSpec((tb, d), lambda i: (i, 0)),
            pl.BlockSpec((tb, LANES), lambda i: (i, 0)),
            pl.BlockSpec((1, d), lambda i: (0, 0)),
        ],
        out_specs=pl.BlockSpec((tb, d), lambda i: (i, 0)),
        out_shape=jax.ShapeDtypeStruct((n, d), F32),
        scratch_shapes=[pltpu.VMEM((tb, d), F32), pltpu.VMEM((tb, d), F32),
                        pltpu.SemaphoreType.DMA(())],
        compiler_params=pltpu.CompilerParams(
            dimension_semantics=("arbitrary",), vmem_limit_bytes=VMEM_LIMIT),
        name="combine",
    )(pos, ys, x3, rw, g)


def _pick_tile(n, pref):
    t = min(pref, n)
    while n % t:
        t //= 2
    return t


def kernel(x, norm_g, pool_w, pool_scale, attn_w_in, attn_b_f, attn_w_out, ffn_w_gate, ffn_w_up, ffn_w_down,
           moe_w_router, moe_w_gate, moe_w_up, moe_w_down):
    b, s, d = x.shape
    n = b * s
    nh = d // HEAD_DIM
    tm = _pick_tile(s, 512)

    x = _layer0(x, norm_g[0], pool_w[0].astype(BF16), pool_scale[0],
                ffn_w_gate[0].astype(BF16), ffn_w_up[0].astype(BF16), ffn_w_down[0].astype(BF16), tm=tm)

    w_in = attn_w_in[0]
    w_f = jnp.zeros((d, LANES), F32).at[:, :nh].set(w_in[:, 3 * d:]).astype(BF16)
    b_f = jnp.zeros((1, LANES), F32).at[0, :nh].set(attn_b_f[0])
    w_qvt = jnp.concatenate([w_in[:, :d], w_in[:, 2 * d:3 * d]], axis=1).T.astype(BF16)
    qt, k, vt, c2, qn, kn = _qkv(x, norm_g[1, 0:1], w_in[:, d:2 * d].astype(BF16), w_qvt, w_f, b_f, tm=tm)
    first = _first_live_block(c2, qn, kn, tq=tm, heads_per_step=LANES // HEAD_DIM)
    o = _attention(qt, k, vt, first, tq=tm)

    w_router = jnp.zeros((d, LANES), F32).at[:, :N_EXPERTS].set(moe_w_router[0])
    x3, ri, rw, counts = _attn_out(o.reshape(n, d), x.reshape(n, d), norm_g[1, 1:3],
                                   attn_w_out[0].astype(BF16), w_router, tm=tm)

    tme = _pick_tile(n, 512)
    counts = counts[0, :N_EXPERTS]
    padded = (counts + tme - 1) // tme * tme
    ends = jnp.cumsum(padded)
    starts = ends - padded
    n_rows = 2 * n + N_EXPERTS * tme
    pos = jnp.stack([starts[ri[:, 0]] + ri[:, 2], starts[ri[:, 1]] + ri[:, 3]], axis=0)
    tb = _pick_tile(n, 256)
    pos = pos.reshape(2, n // tb, tb).transpose(1, 0, 2)
    fill_hi = ends.at[N_EXPERTS - 1].set(n_rows)
    fill = jnp.stack([starts + counts, fill_hi], axis=0).astype(jnp.int32)
    tile_start = jnp.arange(n_rows // tme, dtype=jnp.int32) * tme
    tile_expert = jnp.minimum(jnp.sum(tile_start[:, None] >= ends[None, :], axis=1), N_EXPERTS - 1).astype(jnp.int32)
    n_tiles = (ends[-1:] // tme).astype(jnp.int32)

    xs = _dispatch(x3, norm_g[1, 2:3], pos, fill, n_rows, tb=tb)
    dffe = moe_w_gate.shape[-1]
    tf = dffe // 2 if (dffe // 2) % LANES == 0 else dffe
    ys = _experts(xs, tile_expert, n_tiles, moe_w_gate[0].astype(BF16), moe_w_up[0].astype(BF16),
                  moe_w_down[0].astype(BF16), tm=tme, tf=tf)
    out = _combine(ys, x3, rw, norm_g[1, 3:4], pos, tb=tb)
    return out.reshape(b, s, d)
```

```python
import functools

import jax
import jax.numpy as jnp
from jax import lax
from jax.experimental import pallas as pl
from jax.experimental.pallas import tpu as pltpu

F32 = jnp.float32
BF16 = jnp.bfloat16

RMS_EPS = 1e-6
HEAD_DIM = 64
POOL_WINDOWS = (2, 4, 8, 16)
POOL_HALO = 16
N_EXPERTS = 8
NEG_INF = -1e30
LANES = 128
VMEM_LIMIT = 56 * 1024 * 1024
ISSUE_UNROLL = 8
LOG2E = 1.4426950408889634
ATTN_HEADS_PER_STEP = 4


def _rms(x, g):
    ms = jnp.mean(x * x, axis=-1, keepdims=True)
    return x * lax.rsqrt(ms + RMS_EPS) * g


def _split3(c):
    hi = c.astype(BF16).astype(F32)
    r = c - hi
    mid = r.astype(BF16).astype(F32)
    lo = r - mid
    return hi, mid, lo


def _resident(shape):
    nd = len(shape)
    return pl.BlockSpec(shape, lambda *_: (0,) * nd, pipeline_mode=pl.Buffered(1))


def _layer0_kernel(x_ref, xp_ref, g_ref, pw_ref, ps_ref, wg_ref, wu_ref, wd_ref, o_ref,
                   buf_a, buf_b, *, tm, ff_chunk):
    s = pl.program_id(1)
    d = x_ref.shape[-1]
    gd = d // len(POOL_WINDOWS)
    x = x_ref[0]
    g = g_ref[...]
    h = _rms(x, g[0:1])
    hp = _rms(xp_ref[0], g[0:1])
    hp = jnp.where(s > 0, hp, 0.0)

    lo = 8
    top = tm + POOL_HALO + lo
    zeros8 = jnp.zeros((lo, d), F32)
    buf_a[0:lo, :] = zeros8
    buf_b[0:lo, :] = zeros8
    buf_a[lo:lo + POOL_HALO, :] = hp
    buf_a[lo + POOL_HALO:top, :] = h
    body = lo + POOL_HALO

    def shifted_sum(src, k, c0, rows0):
        n = top - rows0
        return src[rows0:top, c0:] + src[rows0 - k:rows0 - k + n, c0:]

    sums = []
    sums.append((buf_a[body:top, 0:gd] + buf_a[body - 1:top - 1, 0:gd]))
    buf_b[lo:top, gd:] = shifted_sum(buf_a, 1, gd, lo)
    sums.append(buf_b[body:top, gd:2 * gd] + buf_b[body - 2:top - 2, gd:2 * gd])
    buf_a[lo:top, 2 * gd:] = shifted_sum(buf_b, 2, 2 * gd, lo)
    sums.append(buf_a[body:top, 2 * gd:3 * gd] + buf_a[body - 4:top - 4, 2 * gd:3 * gd])
    buf_b[lo:top, 3 * gd:] = shifted_sum(buf_a, 4, 3 * gd, lo)
    sums.append(buf_b[body:top, 3 * gd:] + buf_b[body - 8:top - 8, 3 * gd:])

    pos = s * tm + lax.broadcasted_iota(jnp.int32, (tm, 1), 0)
    mixed = []
    for gi, w in enumerate(POOL_WINDOWS):
        count = jnp.minimum(pos + 1, w).astype(F32)
        pooled = sums[gi] / count - h[:, gi * gd:(gi + 1) * gd]
        mixed.append(jnp.dot(pooled.astype(BF16), pw_ref[gi], preferred_element_type=F32))
    y = jnp.concatenate(mixed, axis=-1) * ps_ref[...]
    x1 = x + _rms(y, g[1:2])

    h2 = _rms(x1, g[2:3]).astype(BF16)
    dff = wg_ref.shape[1]
    acc = jnp.zeros((tm, d), F32)
    for c0 in range(0, dff, ff_chunk):
        gate = jnp.dot(h2, wg_ref[:, c0:c0 + ff_chunk], preferred_element_type=F32)
        up = jnp.dot(h2, wu_ref[:, c0:c0 + ff_chunk], preferred_element_type=F32)
        act = (gate * jax.nn.sigmoid(gate) * up).astype(BF16)
        acc = acc + jnp.dot(act, wd_ref[c0:c0 + ff_chunk, :], preferred_element_type=F32)
    o_ref[0] = x1 + _rms(acc, g[3:4])


def _layer0(x, g4, pool_w, pool_scale, w_gate, w_up, w_down, *, tm):
    b, s, d = x.shape
    dff = w_gate.shape[1]
    ff_chunk = dff // 2 if (dff // 2) % LANES == 0 else dff
    halo_per_tile = tm // POOL_HALO
    kern = functools.partial(_layer0_kernel, tm=tm, ff_chunk=ff_chunk)
    return pl.pallas_call(
        kern,
        grid=(b, s // tm),
        in_specs=[
            pl.BlockSpec((1, tm, d), lambda bi, si: (bi, si, 0)),
            pl.BlockSpec((1, POOL_HALO, d), lambda bi, si: (bi, jnp.maximum(si * halo_per_tile - 1, 0), 0)),
            _resident((4, d)),
            _resident(pool_w.shape),
            _resident((1, d)),
            _resident(w_gate.shape),
            _resident(w_up.shape),
            _resident(w_down.shape),
        ],
        out_specs=pl.BlockSpec((1, tm, d), lambda bi, si: (bi, si, 0)),
        out_shape=jax.ShapeDtypeStruct((b, s, d), F32),
        scratch_shapes=[pltpu.VMEM((tm + POOL_HALO + 8, d), F32),
                        pltpu.VMEM((tm + POOL_HALO + 8, d), F32)],
        compiler_params=pltpu.CompilerParams(
            dimension_semantics=("arbitrary", "arbitrary"), vmem_limit_bytes=VMEM_LIMIT),
        name="layer0",
    )(x, x, g4, pool_w, pool_scale.reshape(1, d), w_gate, w_up, w_down)


def _qkv_kernel(x_ref, g_ref, wk_ref, wqvt_ref, wf_ref, bf_ref, tri_ref, hsel_ref,
                qt_ref, k_ref, vt_ref, c2_ref, qn_ref, kn_ref, carry_ref, *, tm, n_heads):
    s = pl.program_id(1)
    d = x_ref.shape[-1]
    h = _rms(x_ref[0], g_ref[...]).astype(BF16)
    nt = (((1,), (1,)), ((), ()))
    kproj = jnp.dot(h, wk_ref[...], preferred_element_type=F32)
    qvt = lax.dot_general(wqvt_ref[...], h, nt, preferred_element_type=F32)
    z = jnp.dot(h, wf_ref[...], preferred_element_type=F32) + bf_ref[...]
    log_f = jnp.minimum(z, 0.0) - jnp.log(1.0 + jnp.exp(-jnp.abs(z)))

    @pl.when(s == 0)
    def _():
        carry_ref[...] = jnp.zeros_like(carry_ref)

    tri = tri_ref[...]
    c = carry_ref[...]
    for piece in _split3(log_f):
        c = c + jnp.dot(tri, piece.astype(BF16), preferred_element_type=F32)
    carry_ref[...] = c[tm - 1:tm, :]
    c = c * LOG2E
    c2_ref[0] = c
    c_hi, c_mid, c_lo = _split3(c)
    ct_hi, ct_mid, ct_lo = _split3(c.T)

    lane = lax.broadcasted_iota(jnp.int32, (tm, HEAD_DIM), 1)
    row = lax.broadcasted_iota(jnp.int32, (HEAD_DIM, tm), 0)
    aug_vt = jnp.where(row == 0, 1.0, 0.0)
    scale = HEAD_DIM ** -0.5 * LOG2E
    q_sq = jnp.square(qvt[:d, :] * scale).reshape(n_heads, HEAD_DIM, tm)
    qn_ref[0, 0] = jnp.broadcast_to(jnp.max(jnp.sum(q_sq, axis=1), axis=1, keepdims=True), qn_ref.shape[2:])
    k_sq = jnp.dot(jnp.square(kproj).astype(BF16), hsel_ref[...], preferred_element_type=F32)
    kn_ref[0, 0] = jnp.broadcast_to(jnp.max(k_sq, axis=0, keepdims=True), kn_ref.shape[2:])
    for hd in range(n_heads):
        aug_k = jnp.where(lane == 3, -c_hi[:, hd:hd + 1], jnp.where(
            lane == 4, -c_mid[:, hd:hd + 1], jnp.where(
                lane == 5, -c_lo[:, hd:hd + 1], jnp.where(lane < 3, 1.0, 0.0))))
        aug_qt = jnp.where(row == 0, ct_hi[hd:hd + 1, :], jnp.where(
            row == 1, ct_mid[hd:hd + 1, :], jnp.where(
                row == 2, ct_lo[hd:hd + 1, :], jnp.where(row < 6, 1.0, 0.0))))
        c0 = hd * HEAD_DIM
        k_ref[0, hd] = jnp.concatenate([kproj[:, c0:c0 + HEAD_DIM], aug_k], axis=-1).astype(BF16)
        qt_ref[0, hd] = jnp.concatenate([qvt[c0:c0 + HEAD_DIM, :] * scale, aug_qt], axis=0).astype(BF16)
        vt_ref[0, hd] = jnp.concatenate([qvt[d + c0:d + c0 + HEAD_DIM, :], aug_vt], axis=0).astype(BF16)


def _qkv(x, g, w_k, w_qvt, w_f, b_f, *, tm):
    b, s, d = x.shape
    n_heads = d // HEAD_DIM
    dk = 2 * HEAD_DIM
    tri = jnp.tril(jnp.ones((tm, tm), BF16))
    hsel = (jnp.arange(d)[:, None] // HEAD_DIM == jnp.arange(LANES)[None, :]).astype(BF16)
    kern = functools.partial(_qkv_kernel, tm=tm, n_heads=n_heads)
    row_sds = jax.ShapeDtypeStruct((b, n_heads, s, dk), BF16)
    col_sds = jax.ShapeDtypeStruct((b, n_heads, dk, s), BF16)
    row_spec = pl.BlockSpec((1, n_heads, tm, dk), lambda bi, si: (bi, 0, si, 0))
    col_spec = pl.BlockSpec((1, n_heads, dk, tm), lambda bi, si: (bi, 0, 0, si))
    return pl.pallas_call(
        kern,
        grid=(b, s // tm),
        in_specs=[
            pl.BlockSpec((1, tm, d), lambda bi, si: (bi, si, 0)),
            _resident((1, d)),
            _resident(w_k.shape),
            _resident(w_qvt.shape),
            _resident(w_f.shape),
            _resident(b_f.shape),
            _resident((tm, tm)),
            _resident((d, LANES)),
        ],
        out_specs=[col_spec, row_spec, col_spec,
                   pl.BlockSpec((1, tm, LANES), lambda bi, si: (bi, si, 0)),
                   pl.BlockSpec((1, 1, n_heads, LANES), lambda bi, si: (bi, si, 0, 0)),
                   pl.BlockSpec((1, 1, 8, LANES), lambda bi, si: (bi, si, 0, 0))],
        out_shape=[col_sds, row_sds, col_sds,
                   jax.ShapeDtypeStruct((b, s, LANES), F32),
                   jax.ShapeDtypeStruct((b, s // tm, n_heads, LANES), F32),
                   jax.ShapeDtypeStruct((b, s // tm, 8, LANES), F32)],
        scratch_shapes=[pltpu.VMEM((1, LANES), F32)],
        compiler_params=pltpu.CompilerParams(
            dimension_semantics=("arbitrary", "arbitrary"), vmem_limit_bytes=VMEM_LIMIT),
        name="qkv",
    )(x, g, w_k, w_qvt, w_f, b_f, tri, hsel)


def _attn_kernel(first_ref, qt_ref, k_ref, vt_ref, o_ref, m_ref, acc_ref, s0_ref, s1_ref, *, tq, tk, heads_per_step):
    qi = pl.program_id(2)
    step = (pl.program_id(0) * pl.num_programs(1) + pl.program_id(1)) * pl.num_programs(2) + qi
    first = first_ref[step]
    m_ref[...] = jnp.full(m_ref.shape, NEG_INF, F32)
    acc_ref[...] = jnp.zeros(acc_ref.shape, F32)

    def scores(sub, dst_ref):
        start = pl.multiple_of(sub * tk, tk)
        for hh in range(heads_per_step):
            k = k_ref[0, hh, pl.ds(start, tk), :]
            dst_ref[hh] = jnp.dot(k, qt_ref[0, hh], preferred_element_type=F32)

    def consume(sub, src_ref, first_key=None):
        start = pl.multiple_of(sub * tk, tk)
        for hh in range(heads_per_step):
            st = src_ref[hh]
            if first_key is not None:
                key = lax.broadcasted_iota(jnp.int32, (tk, tq), 0) + first_key
                qry = lax.broadcasted_iota(jnp.int32, (tk, tq), 1)
                st = jnp.where(key <= qry, st, NEG_INF)
            m_old = m_ref[hh]
            m_new = jnp.maximum(m_old, jnp.max(st, axis=0, keepdims=True))
            pt = jnp.exp2(st - m_new).astype(BF16)
            alpha = jnp.exp2(m_old - m_new)
            vt = vt_ref[0, hh, :, pl.ds(start, tk)]
            acc_ref[hh] = alpha * acc_ref[hh] + jnp.dot(vt, pt, preferred_element_type=F32)
            m_ref[hh] = m_new

    subs = tq // tk
    assert subs == 2
    scores(2 * first, s0_ref)

    def trip(j, carry):
        scores(2 * j + 1, s1_ref)
        consume(2 * j, s0_ref)
        scores(2 * j + 2, s0_ref)
        consume(2 * j + 1, s1_ref)
        return carry

    lax.fori_loop(first, qi, trip, 0)
    scores(2 * qi + 1, s1_ref)
    consume(2 * qi, s0_ref, first_key=0)
    consume(2 * qi + 1, s1_ref, first_key=tk)
    outs = []
    for hh in range(heads_per_step):
        acc = acc_ref[hh].T
        outs.append(acc[:, :HEAD_DIM] / acc[:, HEAD_DIM:HEAD_DIM + 1])
    o_ref[0] = jnp.concatenate(outs, axis=-1).astype(o_ref.dtype)


UNDERFLOW_LOG2 = 126.0
SKIP_MARGIN_LOG2 = 8.0


def _first_live_block(c2, qn, kn, *, tq, heads_per_step):
    b, s, _ = c2.shape
    nh = qn.shape[2]
    nq = s // tq
    qk = jnp.sqrt(jnp.max(qn[:, :, :, 0], axis=1) * jnp.max(kn[:, :, 0, :nh], axis=1)) * 1.03
    gap = 2.0 * qk + UNDERFLOW_LOG2 + SKIP_MARGIN_LOG2
    c_query = c2[:, 0::tq, :nh]
    c_key = c2[:, tq - 1::tq, :nh]
    dead = (c_key[:, None, :, :] - c_query[:, :, None, :]) > gap[:, None, None, :]
    dead = dead & (jnp.arange(nq)[None, None, :, None] < jnp.arange(nq)[None, :, None, None])
    first = jnp.sum(dead, axis=2).astype(jnp.int32)
    first = jnp.min(first.reshape(b, nq, nh // heads_per_step, heads_per_step), axis=-1)
    return first.transpose(0, 2, 1).reshape(-1)


def _attention(qt, k, vt, first, *, tq):
    b, nh, s, dk = k.shape
    hps = ATTN_HEADS_PER_STEP
    tk = tq // 2
    kern = functools.partial(_attn_kernel, tq=tq, tk=tk, heads_per_step=hps)
    return pl.pallas_call(
        kern,
        grid_spec=pltpu.PrefetchScalarGridSpec(
            num_scalar_prefetch=1,
            grid=(b, nh // hps, s // tq),
            in_specs=[
                pl.BlockSpec((1, hps, dk, tq), lambda bi, hi, qi, f: (bi, hi, 0, qi)),
                pl.BlockSpec((1, hps, s, dk), lambda bi, hi, qi, f: (bi, hi, 0, 0)),
                pl.BlockSpec((1, hps, dk, s), lambda bi, hi, qi, f: (bi, hi, 0, 0)),
            ],
            out_specs=pl.BlockSpec((1, tq, hps * HEAD_DIM), lambda bi, hi, qi, f: (bi, qi, hi)),
            scratch_shapes=[pltpu.VMEM((hps, 1, tq), F32), pltpu.VMEM((hps, dk, tq), F32),
                            pltpu.VMEM((hps, tk, tq), F32), pltpu.VMEM((hps, tk, tq), F32)],
        ),
        out_shape=jax.ShapeDtypeStruct((b, s, nh * HEAD_DIM), BF16),
        compiler_params=pltpu.CompilerParams(
            dimension_semantics=("arbitrary", "arbitrary", "arbitrary"), vmem_limit_bytes=VMEM_LIMIT),
        name="attention",
    )(first, qt, k, vt)


def _attn_out_kernel(o_ref, x_ref, g_ref, wo_ref, wrh_ref, wrl_ref, ltri_ref, x3_ref, ri_ref, rw_ref, cnt_ref,
                     carry_ref, *, tm, sub):
    i = pl.program_id(0)
    g = g_ref[...]

    @pl.when(i == 0)
    def _():
        carry_ref[...] = jnp.zeros_like(carry_ref)

    carry = carry_ref[...]
    lane = lax.broadcasted_iota(jnp.int32, (sub, LANES), 1)
    for r0 in range(0, tm, sub):
        rows = pl.ds(r0, sub)
        y = jnp.dot(o_ref[rows, :], wo_ref[...], preferred_element_type=F32)
        x3 = x_ref[rows, :] + _rms(y, g[0:1])
        x3_ref[rows, :] = x3
        h = _rms(x3, g[1:2])
        h_hi = h.astype(BF16)
        h_lo = (h - h_hi.astype(F32)).astype(BF16)
        logits = (jnp.dot(h_hi, wrh_ref[...], preferred_element_type=F32)
                  + jnp.dot(h_lo, wrh_ref[...], preferred_element_type=F32)
                  + jnp.dot(h_hi, wrl_ref[...], preferred_element_type=F32))
        logits = jnp.where(lane < N_EXPERTS, logits, -jnp.inf)
        m1 = jnp.max(logits, axis=-1, keepdims=True)
        e1 = jnp.min(jnp.where(logits == m1, lane, LANES), axis=-1, keepdims=True)
        rest = jnp.where(lane == e1, -jnp.inf, logits)
        m2 = jnp.max(rest, axis=-1, keepdims=True)
        e2 = jnp.min(jnp.where(rest == m2, lane, LANES), axis=-1, keepdims=True)
        t = jnp.exp(m2 - m1)
        w1 = 1.0 / (1.0 + t)
        w2 = t / (1.0 + t)

        hot1 = lane == e1
        hot2 = lane == e2
        cnt = jnp.where(hot1 | hot2, 1.0, 0.0)
        before = jnp.dot(ltri_ref[...], cnt.astype(BF16), preferred_element_type=F32) + carry
        r1 = jnp.sum(jnp.where(hot1, before, 0.0), axis=-1, keepdims=True)
        r2 = jnp.sum(jnp.where(hot2, before, 0.0), axis=-1, keepdims=True)
        carry = carry + jnp.sum(cnt, axis=0, keepdims=True)
        ri_ref[rows, :] = jnp.where(lane == 0, e1, jnp.where(lane == 1, e2, jnp.where(
            lane == 2, r1.astype(jnp.int32), jnp.where(lane == 3, r2.astype(jnp.int32), 0))))
        rw_ref[rows, :] = jnp.where(lane == 0, w1, jnp.where(lane == 1, w2, 0.0))

    carry_ref[...] = carry
    cnt_ref[...] = jnp.broadcast_to(carry, cnt_ref.shape).astype(jnp.int32)


def _attn_out(o, x, g2, w_out, w_router_pad, *, tm):
    n, d = x.shape
    sub = tm // 2
    ltri = jnp.tril(jnp.ones((sub, sub), BF16), k=-1)
    wr_hi = w_router_pad.astype(BF16)
    wr_lo = (w_router_pad - wr_hi.astype(F32)).astype(BF16)
    kern = functools.partial(_attn_out_kernel, tm=tm, sub=sub)
    row_spec = pl.BlockSpec((tm, d), lambda i: (i, 0))
    meta_spec = pl.BlockSpec((tm, LANES), lambda i: (i, 0))
    return pl.pallas_call(
        kern,
        grid=(n // tm,),
        in_specs=[row_spec, row_spec, _resident((2, d)), _resident(w_out.shape),
                  _resident(wr_hi.shape), _resident(wr_lo.shape), _resident((sub, sub))],
        out_specs=[row_spec, meta_spec, meta_spec, pl.BlockSpec((8, LANES), lambda i: (0, 0))],
        out_shape=[jax.ShapeDtypeStruct((n, d), F32),
                   jax.ShapeDtypeStruct((n, LANES), jnp.int32),
                   jax.ShapeDtypeStruct((n, LANES), F32),
                   jax.ShapeDtypeStruct((8, LANES), jnp.int32)],
        scratch_shapes=[pltpu.VMEM((1, LANES), F32)],
        compiler_params=pltpu.CompilerParams(
            dimension_semantics=("arbitrary",), vmem_limit_bytes=VMEM_LIMIT),
        name="attn_out",
    )(o, x, g2, w_out, wr_hi, wr_lo, ltri)


def _row_copy(src, dst, sem):
    return pltpu.make_async_copy(src, dst, sem)


def _dispatch_kernel(fill_ref, pos_ref, x_ref, g_ref, xs_ref, h_buf, z_buf, sem, zsem, *, tb):
    i = pl.program_id(0)
    last = pl.num_programs(0) - 1
    slot = lax.rem(i, 2)
    h_buf[slot] = _rms(x_ref[...], g_ref[...])

    def issue(t, _):
        for which in range(2):
            p = pos_ref[0, which, t]
            _row_copy(h_buf.at[slot, pl.ds(t, 1), :], xs_ref.at[pl.ds(p, 1), :], sem.at[slot]).start()
        return 0

    lax.fori_loop(0, tb, issue, 0, unroll=ISSUE_UNROLL)

    def drain(s):
        for _ in range(2 * tb):
            _row_copy(h_buf.at[s, pl.ds(0, 1), :], xs_ref.at[pl.ds(0, 1), :], sem.at[s]).wait()

    @pl.when(i == 0)
    def _():
        z_buf[...] = jnp.zeros_like(z_buf)
        for e in range(N_EXPERTS):
            lo = fill_ref[0, e]
            hi = fill_ref[1, e]

            def fill(r, _):
                _row_copy(z_buf, xs_ref.at[pl.ds(r, 1), :], zsem).start()
                return 0

            lax.fori_loop(lo, hi, fill, 0)

            def fill_done(r, _):
                _row_copy(z_buf, xs_ref.at[pl.ds(0, 1), :], zsem).wait()
                return 0

            lax.fori_loop(lo, hi, fill_done, 0)

    @pl.when(i > 0)
    def _():
        drain(1 - slot)

    @pl.when(i == last)
    def _():
        drain(slot)


def _dispatch(x3, g, pos, fill, n_rows, *, tb):
    n, d = x3.shape
    kern = functools.partial(_dispatch_kernel, tb=tb)
    return pl.pallas_call(
        kern,
        grid_spec=pltpu.PrefetchScalarGridSpec(
            num_scalar_prefetch=1,
            grid=(n // tb,),
            in_specs=[
                pl.BlockSpec((1, 2, tb), lambda i, f: (i, 0, 0), memory_space=pltpu.SMEM),
                pl.BlockSpec((tb, d), lambda i, f: (i, 0)),
                pl.BlockSpec((1, d), lambda i, f: (0, 0)),
            ],
            out_specs=pl.BlockSpec(memory_space=pl.ANY),
            scratch_shapes=[pltpu.VMEM((2, tb, d), F32), pltpu.VMEM((1, d), F32),
                            pltpu.SemaphoreType.DMA((2,)), pltpu.SemaphoreType.DMA(())],
        ),
        out_shape=jax.ShapeDtypeStruct((n_rows, d), F32),
        compiler_params=pltpu.CompilerParams(
            dimension_semantics=("arbitrary",), vmem_limit_bytes=VMEM_LIMIT),
        name="dispatch",
    )(fill, pos, x3, g)


def _experts_kernel(te_ref, nt_ref, xs_ref, wg_ref, wu_ref, wd_ref, ys_ref, xb_ref, acc_ref):
    i = pl.program_id(0)
    j = pl.program_id(1)
    nj = pl.num_programs(1)

    @pl.when(i < nt_ref[0])
    def _():
        @pl.when(j == 0)
        def _():
            xb_ref[...] = xs_ref[...].astype(BF16)
            acc_ref[...] = jnp.zeros_like(acc_ref)

        xb = xb_ref[...]
        gate = jnp.dot(xb, wg_ref[0], preferred_element_type=F32)
        up = jnp.dot(xb, wu_ref[0], preferred_element_type=F32)
        act = (gate * jax.nn.sigmoid(gate) * up).astype(BF16)
        acc_ref[...] += jnp.dot(act, wd_ref[0], preferred_element_type=F32)

        @pl.when(j == nj - 1)
        def _():
            ys_ref[...] = acc_ref[...]

    @pl.when((i >= nt_ref[0]) & (j == nj - 1))
    def _():
        ys_ref[...] = jnp.zeros_like(ys_ref)


def _experts(xs, tile_expert, n_tiles, w_gate, w_up, w_down, *, tm, tf):
    n_rows, d = xs.shape
    dff = w_gate.shape[-1]
    max_tiles = n_rows // tm

    def row_map(i, j, te, nt):
        return (jnp.minimum(i, nt[0] - 1), 0)

    def wcol_map(i, j, te, nt):
        return (te[i], 0, j)

    def wrow_map(i, j, te, nt):
        return (te[i], j, 0)

    return pl.pallas_call(
        _experts_kernel,
        grid_spec=pltpu.PrefetchScalarGridSpec(
            num_scalar_prefetch=2,
            grid=(max_tiles, dff // tf),
            in_specs=[
                pl.BlockSpec((tm, d), row_map),
                pl.BlockSpec((1, d, tf), wcol_map),
                pl.BlockSpec((1, d, tf), wcol_map),
                pl.BlockSpec((1, tf, d), wrow_map),
            ],
            out_specs=pl.BlockSpec((tm, d), lambda i, j, te, nt: (i, 0)),
            scratch_shapes=[pltpu.VMEM((tm, d), BF16), pltpu.VMEM((tm, d), F32)],
        ),
        out_shape=jax.ShapeDtypeStruct((n_rows, d), F32),
        compiler_params=pltpu.CompilerParams(
            dimension_semantics=("arbitrary", "arbitrary"), vmem_limit_bytes=VMEM_LIMIT),
        name="experts",
    )(tile_expert, n_tiles, xs, w_gate, w_up, w_down)


def _combine_kernel(pos_ref, posn_ref, ys_ref, x_ref, rw_ref, g_ref, o_ref, a_buf, b_buf, sem, *, tb):
    i = pl.program_id(0)
    last = pl.num_programs(0) - 1
    slot = lax.rem(i, 2)

    def gather(idx_ref, s):
        def issue(t, _):
            _row_copy(ys_ref.at[pl.ds(idx_ref[0, 0, t], 1), :], a_buf.at[s, pl.ds(t, 1), :], sem.at[s]).start()
            _row_copy(ys_ref.at[pl.ds(idx_ref[0, 1, t], 1), :], b_buf.at[s, pl.ds(t, 1), :], sem.at[s]).start()
            return 0

        lax.fori_loop(0, tb, issue, 0, unroll=ISSUE_UNROLL)

    @pl.when(i == 0)
    def _():
        gather(pos_ref, slot)

    @pl.when(i < last)
    def _():
        gather(posn_ref, 1 - slot)

    for _ in range(2 * tb):
        _row_copy(ys_ref.at[pl.ds(0, 1), :], a_buf.at[slot, pl.ds(0, 1), :], sem.at[slot]).wait()
    rw = rw_ref[...]
    y = rw[:, 0:1] * a_buf[slot] + rw[:, 1:2] * b_buf[slot]
    o_ref[...] = x_ref[...] + _rms(y, g_ref[...])


def _combine(ys, x3, rw, g, pos, *, tb):
    n, d = x3.shape
    n_tiles = n // tb
    kern = functools.partial(_combine_kernel, tb=tb)
    return pl.pallas_call(
        kern,
        grid=(n_tiles,),
        in_specs=[
            pl.BlockSpec((1, 2, tb), lambda i: (i, 0, 0), memory_space=pltpu.SMEM),
            pl.BlockSpec((1, 2, tb), lambda i: (jnp.minimum(i + 1, n_tiles - 1), 0, 0), memory_space=pltpu.SMEM),
            pl.BlockSpec(memory_space=pl.ANY),
            pl.BlockSpec((tb, d), lambda i: (i, 0)),
            pl.BlockSpec((tb, LANES), lambda i: (i, 0)),
            pl.BlockSpec((1, d), lambda i: (0, 0)),
        ],
        out_specs=pl.BlockSpec((tb, d), lambda i: (i, 0)),
        out_shape=jax.ShapeDtypeStruct((n, d), F32),
        scratch_shapes=[pltpu.VMEM((2, tb, d), F32), pltpu.VMEM((2, tb, d), F32),
                        pltpu.SemaphoreType.DMA((2,))],
        compiler_params=pltpu.CompilerParams(
            dimension_semantics=("arbitrary",), vmem_limit_bytes=VMEM_LIMIT),
        name="combine",
    )(pos, pos, ys, x3, rw, g)


def _pick_tile(n, pref):
    t = min(pref, n)
    while n % t:
        t //= 2
    return t


def kernel(x, norm_g, pool_w, pool_scale, attn_w_in, attn_b_f, attn_w_out, ffn_w_gate, ffn_w_up, ffn_w_down,
           moe_w_router, moe_w_gate, moe_w_up, moe_w_down):
    b, s, d = x.shape
    n = b * s
    nh = d // HEAD_DIM
    tm = _pick_tile(s, 512)

    x = _layer0(x, norm_g[0], pool_w[0].astype(BF16), pool_scale[0],
                ffn_w_gate[0].astype(BF16), ffn_w_up[0].astype(BF16), ffn_w_down[0].astype(BF16), tm=tm)

    w_in = attn_w_in[0]
    w_f = jnp.zeros((d, LANES), F32).at[:, :nh].set(w_in[:, 3 * d:]).astype(BF16)
    b_f = jnp.zeros((1, LANES), F32).at[0, :nh].set(attn_b_f[0])
    w_qvt = jnp.concatenate([w_in[:, :d], w_in[:, 2 * d:3 * d]], axis=1).T.astype(BF16)
    qt, k, vt, c2, qn, kn = _qkv(x, norm_g[1, 0:1], w_in[:, d:2 * d].astype(BF16), w_qvt, w_f, b_f, tm=tm)
    first = _first_live_block(c2, qn, kn, tq=tm, heads_per_step=ATTN_HEADS_PER_STEP)
    o = _attention(qt, k, vt, first, tq=tm)

    w_router = jnp.zeros((d, LANES), F32).at[:, :N_EXPERTS].set(moe_w_router[0])
    x3, ri, rw, counts = _attn_out(o.reshape(n, d), x.reshape(n, d), norm_g[1, 1:3],
                                   attn_w_out[0].astype(BF16), w_router, tm=tm)

    tme = _pick_tile(n, 512)
    counts = counts[0, :N_EXPERTS]
    padded = (counts + tme - 1) // tme * tme
    ends = jnp.cumsum(padded)
    starts = ends - padded
    n_rows = 2 * n + N_EXPERTS * tme
    pos = jnp.stack([starts[ri[:, 0]] + ri[:, 2], starts[ri[:, 1]] + ri[:, 3]], axis=0)
    tb = _pick_tile(n, 256)
    pos = pos.reshape(2, n // tb, tb).transpose(1, 0, 2)
    fill_hi = ends.at[N_EXPERTS - 1].set(n_rows)
    fill = jnp.stack([starts + counts, fill_hi], axis=0).astype(jnp.int32)
    tile_start = jnp.arange(n_rows // tme, dtype=jnp.int32) * tme
    tile_expert = jnp.minimum(jnp.sum(tile_start[:, None] >= ends[None, :], axis=1), N_EXPERTS - 1).astype(jnp.int32)
    n_tiles = (ends[-1:] // tme).astype(jnp.int32)

    xs = _dispatch(x3, norm_g[1, 2:3], pos, fill, n_rows, tb=tb)
    dffe = moe_w_gate.shape[-1]
    tf = dffe // 2 if (dffe // 2) % LANES == 0 else dffe
    ys = _experts(xs, tile_expert, n_tiles, moe_w_gate[0].astype(BF16), moe_w_up[0].astype(BF16),
                  moe_w_down[0].astype(BF16), tm=tme, tf=tf)
    out = _combine(ys, x3, rw, norm_g[1, 3:4], pos, tb=tb)
    return out.reshape(b, s, d)
```

```python
import functools

import jax
import jax.numpy as jnp
from jax import lax
from jax.experimental import pallas as pl
from jax.experimental.pallas import tpu as pltpu

F32 = jnp.float32
BF16 = jnp.bfloat16

RMS_EPS = 1e-6
HEAD_DIM = 64
POOL_WINDOWS = (2, 4, 8, 16)
POOL_HALO = 16
N_EXPERTS = 8
NEG_INF = -1e30
LANES = 128
VMEM_LIMIT = 56 * 1024 * 1024
ISSUE_UNROLL = 8
LOG2E = 1.4426950408889634
ATTN_HEADS_PER_STEP = 4
ROW_CHAINS = 2


def _rms(x, g):
    ms = jnp.mean(x * x, axis=-1, keepdims=True)
    return x * lax.rsqrt(ms + RMS_EPS) * g


def _split3(c):
    hi = c.astype(BF16).astype(F32)
    r = c - hi
    mid = r.astype(BF16).astype(F32)
    lo = r - mid
    return hi, mid, lo


def _resident(shape):
    nd = len(shape)
    return pl.BlockSpec(shape, lambda *_: (0,) * nd, pipeline_mode=pl.Buffered(1))


def _layer0_kernel(x_ref, xp_ref, g_ref, pw_ref, ps_ref, wg_ref, wu_ref, wd_ref, o_ref,
                   buf_a, buf_b, *, tm, sub, ff_chunk):
    s = pl.program_id(1)
    d = x_ref.shape[-1]
    gd = d // len(POOL_WINDOWS)
    g = g_ref[...]
    dff = wg_ref.shape[1]
    lo = 8
    top = sub + POOL_HALO + lo
    body = lo + POOL_HALO
    zeros8 = jnp.zeros((lo, d), F32)
    halo = jnp.where(s > 0, _rms(xp_ref[0], g[0:1]), 0.0)

    for idx, r0 in enumerate(range(0, tm, sub)):
        x = x_ref[0, r0:r0 + sub, :]
        h = _rms(x, g[0:1])
        ba = buf_a.at[idx]
        bb = buf_b.at[idx]
        ba[0:lo, :] = zeros8
        bb[0:lo, :] = zeros8
        ba[lo:body, :] = halo
        ba[body:top, :] = h
        halo = h[sub - POOL_HALO:, :]

        def shifted_sum(src, k, c0):
            return src[lo:top, c0:] + src[lo - k:top - k, c0:]

        sums = [ba[body:top, 0:gd] + ba[body - 1:top - 1, 0:gd]]
        bb[lo:top, gd:] = shifted_sum(ba, 1, gd)
        sums.append(bb[body:top, gd:2 * gd] + bb[body - 2:top - 2, gd:2 * gd])
        ba[lo:top, 2 * gd:] = shifted_sum(bb, 2, 2 * gd)
        sums.append(ba[body:top, 2 * gd:3 * gd] + ba[body - 4:top - 4, 2 * gd:3 * gd])
        bb[lo:top, 3 * gd:] = shifted_sum(ba, 4, 3 * gd)
        sums.append(bb[body:top, 3 * gd:] + bb[body - 8:top - 8, 3 * gd:])

        pos = s * tm + r0 + lax.broadcasted_iota(jnp.int32, (sub, 1), 0)
        mixed = []
        for gi, w in enumerate(POOL_WINDOWS):
            count = jnp.minimum(pos + 1, w).astype(F32)
            pooled = sums[gi] / count - h[:, gi * gd:(gi + 1) * gd]
            mixed.append(jnp.dot(pooled.astype(BF16), pw_ref[gi], preferred_element_type=F32))
        y = jnp.concatenate(mixed, axis=-1) * ps_ref[...]
        x1 = x + _rms(y, g[1:2])

        h2 = _rms(x1, g[2:3]).astype(BF16)
        acc = jnp.zeros((sub, d), F32)
        for c0 in range(0, dff, ff_chunk):
            gate = jnp.dot(h2, wg_ref[:, c0:c0 + ff_chunk], preferred_element_type=F32)
            up = jnp.dot(h2, wu_ref[:, c0:c0 + ff_chunk], preferred_element_type=F32)
            act = (gate * jax.nn.sigmoid(gate) * up).astype(BF16)
            acc = acc + jnp.dot(act, wd_ref[c0:c0 + ff_chunk, :], preferred_element_type=F32)
        o_ref[0, r0:r0 + sub, :] = x1 + _rms(acc, g[3:4])


def _layer0(x, g4, pool_w, pool_scale, w_gate, w_up, w_down, *, tm):
    b, s, d = x.shape
    dff = w_gate.shape[1]
    ff_chunk = dff // 2 if (dff // 2) % LANES == 0 else dff
    halo_per_tile = tm // POOL_HALO
    sub = tm // ROW_CHAINS
    kern = functools.partial(_layer0_kernel, tm=tm, sub=sub, ff_chunk=ff_chunk)
    return pl.pallas_call(
        kern,
        grid=(b, s // tm),
        in_specs=[
            pl.BlockSpec((1, tm, d), lambda bi, si: (bi, si, 0)),
            pl.BlockSpec((1, POOL_HALO, d), lambda bi, si: (bi, jnp.maximum(si * halo_per_tile - 1, 0), 0)),
            _resident((4, d)),
            _resident(pool_w.shape),
            _resident((1, d)),
            _resident(w_gate.shape),
            _resident(w_up.shape),
            _resident(w_down.shape),
        ],
        out_specs=pl.BlockSpec((1, tm, d), lambda bi, si: (bi, si, 0)),
        out_shape=jax.ShapeDtypeStruct((b, s, d), F32),
        scratch_shapes=[pltpu.VMEM((ROW_CHAINS, sub + POOL_HALO + 8, d), F32),
                        pltpu.VMEM((ROW_CHAINS, sub + POOL_HALO + 8, d), F32)],
        compiler_params=pltpu.CompilerParams(
            dimension_semantics=("arbitrary", "arbitrary"), vmem_limit_bytes=VMEM_LIMIT),
        name="layer0",
    )(x, x, g4, pool_w, pool_scale.reshape(1, d), w_gate, w_up, w_down)


def _qkv_kernel(x_ref, g_ref, wk_ref, wqvt_ref, wf_ref, bf_ref, tri_ref, hsel_ref,
                qt_ref, k_ref, vt_ref, c2_ref, qn_ref, kn_ref, carry_ref, *, tm, n_heads):
    s = pl.program_id(1)
    d = x_ref.shape[-1]
    nt = (((1,), (1,)), ((), ()))

    @pl.when(s == 0)
    def _():
        carry_ref[...] = jnp.zeros_like(carry_ref)

    sub = tm // ROW_CHAINS
    lane = lax.broadcasted_iota(jnp.int32, (sub, HEAD_DIM), 1)
    row = lax.broadcasted_iota(jnp.int32, (HEAD_DIM, sub), 0)
    aug_vt = jnp.where(row == 0, 1.0, 0.0)
    scale = HEAD_DIM ** -0.5 * LOG2E
    tri = tri_ref[...]
    carry = carry_ref[...]
    qn_max = None
    kn_max = None
    for r0 in range(0, tm, sub):
        rows = pl.ds(r0, sub)
        h = _rms(x_ref[0, rows, :], g_ref[...]).astype(BF16)
        kproj = jnp.dot(h, wk_ref[...], preferred_element_type=F32)
        qvt = lax.dot_general(wqvt_ref[...], h, nt, preferred_element_type=F32)
        z = jnp.dot(h, wf_ref[...], preferred_element_type=F32) + bf_ref[...]
        log_f = jnp.minimum(z, 0.0) - jnp.log(1.0 + jnp.exp(-jnp.abs(z)))

        c = carry
        for piece in _split3(log_f):
            c = c + jnp.dot(tri, piece.astype(BF16), preferred_element_type=F32)
        carry = c[sub - 1:sub, :]
        c = c * LOG2E
        c2_ref[0, rows, :] = c
        c_hi, c_mid, c_lo = _split3(c)
        ct_hi, ct_mid, ct_lo = _split3(c.T)

        q_sq = jnp.square(qvt[:d, :] * scale).reshape(n_heads, HEAD_DIM, sub)
        qn = jnp.max(jnp.sum(q_sq, axis=1), axis=1, keepdims=True)
        k_sq = jnp.dot(jnp.square(kproj).astype(BF16), hsel_ref[...], preferred_element_type=F32)
        kn = jnp.max(k_sq, axis=0, keepdims=True)
        qn_max = qn if qn_max is None else jnp.maximum(qn_max, qn)
        kn_max = kn if kn_max is None else jnp.maximum(kn_max, kn)
        for hd in range(n_heads):
            aug_k = jnp.where(lane == 3, -c_hi[:, hd:hd + 1], jnp.where(
                lane == 4, -c_mid[:, hd:hd + 1], jnp.where(
                    lane == 5, -c_lo[:, hd:hd + 1], jnp.where(lane < 3, 1.0, 0.0))))
            aug_qt = jnp.where(row == 0, ct_hi[hd:hd + 1, :], jnp.where(
                row == 1, ct_mid[hd:hd + 1, :], jnp.where(
                    row == 2, ct_lo[hd:hd + 1, :], jnp.where(row < 6, 1.0, 0.0))))
            c0 = hd * HEAD_DIM
            k_ref[0, hd, rows, :] = jnp.concatenate([kproj[:, c0:c0 + HEAD_DIM], aug_k], axis=-1).astype(BF16)
            qt_ref[0, hd, :, rows] = jnp.concatenate(
                [qvt[c0:c0 + HEAD_DIM, :] * scale, aug_qt], axis=0).astype(BF16)
            vt_ref[0, hd, :, rows] = jnp.concatenate(
                [qvt[d + c0:d + c0 + HEAD_DIM, :], aug_vt], axis=0).astype(BF16)

    carry_ref[...] = carry
    qn_ref[0, 0] = jnp.broadcast_to(qn_max, qn_ref.shape[2:])
    kn_ref[0, 0] = jnp.broadcast_to(kn_max, kn_ref.shape[2:])


def _qkv(x, g, w_k, w_qvt, w_f, b_f, *, tm):
    b, s, d = x.shape
    n_heads = d // HEAD_DIM
    dk = 2 * HEAD_DIM
    sub = tm // ROW_CHAINS
    tri = jnp.tril(jnp.ones((sub, sub), BF16))
    hsel =(jnp.arange(d)[:, None] // HEAD_DIM == jnp.arange(LANES)[None, :]).astype(BF16)
    kern = functools.partial(_qkv_kernel, tm=tm, n_heads=n_heads)
    row_sds = jax.ShapeDtypeStruct((b, n_heads, s, dk), BF16)
    col_sds = jax.ShapeDtypeStruct((b, n_heads, dk, s), BF16)
    row_spec = pl.BlockSpec((1, n_heads, tm, dk), lambda bi, si: (bi, 0, si, 0))
    col_spec = pl.BlockSpec((1, n_heads, dk, tm), lambda bi, si: (bi, 0, 0, si))
    return pl.pallas_call(
        kern,
        grid=(b, s // tm),
        in_specs=[
            pl.BlockSpec((1, tm, d), lambda bi, si: (bi, si, 0)),
            _resident((1, d)),
            _resident(w_k.shape),
            _resident(w_qvt.shape),
            _resident(w_f.shape),
            _resident(b_f.shape),
            _resident((sub, sub)),
            _resident((d, LANES)),
        ],
        out_specs=[col_spec, row_spec, col_spec,
                   pl.BlockSpec((1, tm, LANES), lambda bi, si: (bi, si, 0)),
                   pl.BlockSpec((1, 1, n_heads, LANES), lambda bi, si: (bi, si, 0, 0)),
                   pl.BlockSpec((1, 1, 8, LANES), lambda bi, si: (bi, si, 0, 0))],
        out_shape=[col_sds, row_sds, col_sds,
                   jax.ShapeDtypeStruct((b, s, LANES), F32),
                   jax.ShapeDtypeStruct((b, s // tm, n_heads, LANES), F32),
                   jax.ShapeDtypeStruct((b, s // tm, 8, LANES), F32)],
        scratch_shapes=[pltpu.VMEM((1, LANES), F32)],
        compiler_params=pltpu.CompilerParams(
            dimension_semantics=("arbitrary", "arbitrary"), vmem_limit_bytes=VMEM_LIMIT),
        name="qkv",
    )(x, g, w_k, w_qvt, w_f, b_f, tri, hsel)


def _attn_kernel(first_ref, qt_ref, k_ref, vt_ref, eye_ref, o_ref, m_ref, acc_ref, s0_ref, s1_ref,
                 *, tq, tk, heads_per_step):
    qi = pl.program_id(2)
    step = (pl.program_id(0) * pl.num_programs(1) + pl.program_id(1)) * pl.num_programs(2) + qi
    first = first_ref[step]
    m_ref[...] = jnp.full(m_ref.shape, NEG_INF, F32)
    acc_ref[...] = jnp.zeros(acc_ref.shape, F32)

    def scores(sub, dst_ref):
        start = pl.multiple_of(sub * tk, tk)
        for hh in range(heads_per_step):
            k = k_ref[0, hh, pl.ds(start, tk), :]
            dst_ref[hh] = jnp.dot(k, qt_ref[0, hh], preferred_element_type=F32)

    def consume(sub, src_ref, first_key=None):
        start = pl.multiple_of(sub * tk, tk)
        for hh in range(heads_per_step):
            st = src_ref[hh]
            if first_key is not None:
                key = lax.broadcasted_iota(jnp.int32, (tk, tq), 0) + first_key
                qry = lax.broadcasted_iota(jnp.int32, (tk, tq), 1)
                st = jnp.where(key <= qry, st, NEG_INF)
            m_old = m_ref[hh]
            m_new = jnp.maximum(m_old, jnp.max(st, axis=0, keepdims=True))
            pt = jnp.exp2(st - m_new).astype(BF16)
            alpha = jnp.exp2(m_old - m_new)
            vt = vt_ref[0, hh, :, pl.ds(start, tk)]
            acc_ref[hh] = alpha * acc_ref[hh] + jnp.dot(vt, pt, preferred_element_type=F32)
            m_ref[hh] = m_new

    subs = tq // tk
    assert subs == 2
    scores(2 * first, s0_ref)

    def trip(j, carry):
        scores(2 * j + 1, s1_ref)
        consume(2 * j, s0_ref)
        scores(2 * j + 2, s0_ref)
        consume(2 * j + 1, s1_ref)
        return carry

    lax.fori_loop(first, qi, trip, 0)
    scores(2 * qi + 1, s1_ref)
    consume(2 * qi, s0_ref, first_key=0)
    consume(2 * qi + 1, s1_ref, first_key=tk)
    outs = []
    for hh in range(heads_per_step):
        acc = acc_ref[hh]
        outs.append((acc[:HEAD_DIM, :] / acc[HEAD_DIM:HEAD_DIM + 1, :]).astype(o_ref.dtype))
    o_t = jnp.concatenate(outs, axis=0)
    o_ref[0] = lax.dot_general(eye_ref[...], o_t, (((1,), (1,)), ((), ())),
                               preferred_element_type=F32).astype(o_ref.dtype)


UNDERFLOW_LOG2 = 126.0
SKIP_MARGIN_LOG2 = 8.0


def _first_live_block(c2, qn, kn, *, tq, heads_per_step):
    b, s, _ = c2.shape
    nh = qn.shape[2]
    nq = s // tq
    qk = jnp.sqrt(jnp.max(qn[:, :, :, 0], axis=1) * jnp.max(kn[:, :, 0, :nh], axis=1)) * 1.03
    gap = 2.0 * qk + UNDERFLOW_LOG2 + SKIP_MARGIN_LOG2
    c_query = c2[:, 0::tq, :nh]
    c_key = c2[:, tq - 1::tq, :nh]
    dead = (c_key[:, None, :, :] - c_query[:, :, None, :]) > gap[:, None, None, :]
    dead = dead & (jnp.arange(nq)[None, None, :, None] < jnp.arange(nq)[None, :, None, None])
    first = jnp.sum(dead, axis=2).astype(jnp.int32)
    first = jnp.min(first.reshape(b, nq, nh // heads_per_step, heads_per_step), axis=-1)
    return first.transpose(0, 2, 1).reshape(-1)


def _attention(qt, k, vt, first, *, tq):
    b, nh, s, dk = k.shape
    hps = ATTN_HEADS_PER_STEP
    tk = tq // 2
    kern = functools.partial(_attn_kernel, tq=tq, tk=tk, heads_per_step=hps)
    return pl.pallas_call(
        kern,
        grid_spec=pltpu.PrefetchScalarGridSpec(
            num_scalar_prefetch=1,
            grid=(b, nh // hps, s // tq),
            in_specs=[
                pl.BlockSpec((1, hps, dk, tq), lambda bi, hi, qi, f: (bi, hi, 0, qi)),
                pl.BlockSpec((1, hps, s, dk), lambda bi, hi, qi, f: (bi, hi, 0, 0)),
                pl.BlockSpec((1, hps, dk, s), lambda bi, hi, qi, f: (bi, hi, 0, 0)),
                pl.BlockSpec((tq, tq), lambda bi, hi, qi, f: (0, 0), pipeline_mode=pl.Buffered(1)),
            ],
            out_specs=pl.BlockSpec((1, tq, hps * HEAD_DIM), lambda bi, hi, qi, f: (bi, qi, hi)),
            scratch_shapes=[pltpu.VMEM((hps, 1, tq), F32), pltpu.VMEM((hps, dk, tq), F32),
                            pltpu.VMEM((hps, tk, tq), F32), pltpu.VMEM((hps, tk, tq), F32)],
        ),
        out_shape=jax.ShapeDtypeStruct((b, s, nh * HEAD_DIM), BF16),
        compiler_params=pltpu.CompilerParams(
            dimension_semantics=("arbitrary", "arbitrary", "arbitrary"), vmem_limit_bytes=VMEM_LIMIT),
        name="attention",
    )(first, qt, k, vt, jnp.eye(tq, dtype=BF16))


def _attn_out_kernel(o_ref, x_ref, g_ref, wo_ref, wrh_ref, wrl_ref, ltri_ref, x3_ref, ri_ref, rw_ref, cnt_ref,
                     carry_ref, *, tm, sub):
    i = pl.program_id(0)
    g = g_ref[...]

    @pl.when(i == 0)
    def _():
        carry_ref[...] = jnp.zeros_like(carry_ref)

    carry = carry_ref[...]
    lane = lax.broadcasted_iota(jnp.int32, (sub, LANES), 1)
    for r0 in range(0, tm, sub):
        rows = pl.ds(r0, sub)
        y = jnp.dot(o_ref[rows, :], wo_ref[...], preferred_element_type=F32)
        x3 = x_ref[rows, :] + _rms(y, g[0:1])
        x3_ref[rows, :] = x3
        h = _rms(x3, g[1:2])
        h_hi = h.astype(BF16)
        h_lo = (h - h_hi.astype(F32)).astype(BF16)
        logits = (jnp.dot(h_hi, wrh_ref[...], preferred_element_type=F32)
                  + jnp.dot(h_lo, wrh_ref[...], preferred_element_type=F32)
                  + jnp.dot(h_hi, wrl_ref[...], preferred_element_type=F32))
        logits = jnp.where(lane < N_EXPERTS, logits, -jnp.inf)
        m1 = jnp.max(logits, axis=-1, keepdims=True)
        e1 = jnp.min(jnp.where(logits == m1, lane, LANES), axis=-1, keepdims=True)
        rest = jnp.where(lane == e1, -jnp.inf, logits)
        m2 = jnp.max(rest, axis=-1, keepdims=True)
        e2 = jnp.min(jnp.where(rest == m2, lane, LANES), axis=-1, keepdims=True)
        t = jnp.exp(m2 - m1)
        w1 = 1.0 / (1.0 + t)
        w2 = t / (1.0 + t)

        hot1 = lane == e1
        hot2 = lane == e2
        cnt = jnp.where(hot1 | hot2, 1.0, 0.0)
        before = jnp.dot(ltri_ref[...], cnt.astype(BF16), preferred_element_type=F32) + carry
        r1 = jnp.sum(jnp.where(hot1, before, 0.0), axis=-1, keepdims=True)
        r2 = jnp.sum(jnp.where(hot2, before, 0.0), axis=-1, keepdims=True)
        carry = carry + jnp.sum(cnt, axis=0, keepdims=True)
        ri_ref[rows, :] = jnp.where(lane == 0, e1, jnp.where(lane == 1, e2, jnp.where(
            lane == 2, r1.astype(jnp.int32), jnp.where(lane == 3, r2.astype(jnp.int32), 0))))
        rw_ref[rows, :] = jnp.where(lane == 0, w1, jnp.where(lane == 1, w2, 0.0))

    carry_ref[...] = carry
    cnt_ref[...] = jnp.broadcast_to(carry, cnt_ref.shape).astype(jnp.int32)


def _attn_out(o, x, g2, w_out, w_router_pad, *, tm):
    n, d = x.shape
    sub = tm // 2
    ltri = jnp.tril(jnp.ones((sub, sub), BF16), k=-1)
    wr_hi = w_router_pad.astype(BF16)
    wr_lo = (w_router_pad - wr_hi.astype(F32)).astype(BF16)
    kern = functools.partial(_attn_out_kernel, tm=tm, sub=sub)
    row_spec = pl.BlockSpec((tm, d), lambda i: (i, 0))
    meta_spec = pl.BlockSpec((tm, LANES), lambda i: (i, 0))
    return pl.pallas_call(
        kern,
        grid=(n // tm,),
        in_specs=[row_spec, row_spec, _resident((2, d)), _resident(w_out.shape),
                  _resident(wr_hi.shape), _resident(wr_lo.shape), _resident((sub, sub))],
        out_specs=[row_spec, meta_spec, meta_spec, pl.BlockSpec((8, LANES), lambda i: (0, 0))],
        out_shape=[jax.ShapeDtypeStruct((n, d), F32),
                   jax.ShapeDtypeStruct((n, LANES), jnp.int32),
                   jax.ShapeDtypeStruct((n, LANES), F32),
                   jax.ShapeDtypeStruct((8, LANES), jnp.int32)],
        scratch_shapes=[pltpu.VMEM((1, LANES), F32)],
        compiler_params=pltpu.CompilerParams(
            dimension_semantics=("arbitrary",), vmem_limit_bytes=VMEM_LIMIT),
        name="attn_out",
    )(o, x, g2, w_out, wr_hi, wr_lo, ltri)


def _row_copy(src, dst, sem):
    return pltpu.make_async_copy(src, dst, sem)


def _dispatch_kernel(fill_ref, pos_ref, x_ref, g_ref, xs_ref, h_buf, z_buf, sem, zsem, *, tb):
    i = pl.program_id(0)
    last = pl.num_programs(0) - 1
    slot = lax.rem(i, 2)
    h_buf[slot] = _rms(x_ref[...], g_ref[...])

    def issue(t, _):
        for which in range(2):
            p = pos_ref[0, which, t]
            _row_copy(h_buf.at[slot, pl.ds(t, 1), :], xs_ref.at[pl.ds(p, 1), :], sem.at[slot]).start()
        return 0

    lax.fori_loop(0, tb, issue, 0, unroll=ISSUE_UNROLL)

    def drain(s):
        for _ in range(2 * tb):
            _row_copy(h_buf.at[s, pl.ds(0, 1), :], xs_ref.at[pl.ds(0, 1), :], sem.at[s]).wait()

    @pl.when(i == 0)
    def _():
        z_buf[...] = jnp.zeros_like(z_buf)
        for e in range(N_EXPERTS):
            lo = fill_ref[0, e]
            hi = fill_ref[1, e]

            def fill(r, _):
                _row_copy(z_buf, xs_ref.at[pl.ds(r, 1), :], zsem).start()
                return 0

            lax.fori_loop(lo, hi, fill, 0)

            def fill_done(r, _):
                _row_copy(z_buf, xs_ref.at[pl.ds(0, 1), :], zsem).wait()
                return 0

            lax.fori_loop(lo, hi, fill_done, 0)

    @pl.when(i > 0)
    def _():
        drain(1 - slot)

    @pl.when(i == last)
    def _():
        drain(slot)


def _dispatch(x3, g, pos, fill, n_rows, *, tb):
    n, d = x3.shape
    kern = functools.partial(_dispatch_kernel, tb=tb)
    return pl.pallas_call(
        kern,
        grid_spec=pltpu.PrefetchScalarGridSpec(
            num_scalar_prefetch=1,
            grid=(n // tb,),
            in_specs=[
                pl.BlockSpec((1, 2, tb), lambda i, f: (i, 0, 0), memory_space=pltpu.SMEM),
                pl.BlockSpec((tb, d), lambda i, f: (i, 0)),
                pl.BlockSpec((1, d), lambda i, f: (0, 0)),
            ],
            out_specs=pl.BlockSpec(memory_space=pl.ANY),
            scratch_shapes=[pltpu.VMEM((2, tb, d), F32), pltpu.VMEM((1, d), F32),
                            pltpu.SemaphoreType.DMA((2,)), pltpu.SemaphoreType.DMA(())],
        ),
        out_shape=jax.ShapeDtypeStruct((n_rows, d), F32),
        compiler_params=pltpu.CompilerParams(
            dimension_semantics=("arbitrary",), vmem_limit_bytes=VMEM_LIMIT),
        name="dispatch",
    )(fill, pos, x3, g)


def _experts_kernel(te_ref, nt_ref, xs_ref, wg_ref, wu_ref, wd_ref, ys_ref, xb_ref, acc_ref):
    i = pl.program_id(0)
    j = pl.program_id(1)
    nj = pl.num_programs(1)

    @pl.when(i < nt_ref[0])
    def _():
        @pl.when(j == 0)
        def _():
            xb_ref[...] = xs_ref[...].astype(BF16)
            acc_ref[...] = jnp.zeros_like(acc_ref)

        sub = xb_ref.shape[0] // ROW_CHAINS
        for r0 in range(0, xb_ref.shape[0], sub):
            xb = xb_ref[r0:r0 + sub, :]
            gate = jnp.dot(xb, wg_ref[0], preferred_element_type=F32)
            up = jnp.dot(xb, wu_ref[0], preferred_element_type=F32)
            act = (gate * jax.nn.sigmoid(gate) * up).astype(BF16)
            acc_ref[r0:r0 + sub, :] += jnp.dot(act, wd_ref[0], preferred_element_type=F32)

        @pl.when(j == nj - 1)
        def _():
            ys_ref[...] = acc_ref[...]

    @pl.when((i >= nt_ref[0]) & (j == nj - 1))
    def _():
        ys_ref[...] = jnp.zeros_like(ys_ref)


def _experts(xs, tile_expert, n_tiles, w_gate, w_up, w_down, *, tm, tf):
    n_rows, d = xs.shape
    dff = w_gate.shape[-1]
    max_tiles = n_rows // tm

    def row_map(i, j, te, nt):
        return (jnp.minimum(i, nt[0] - 1), 0)

    def wcol_map(i, j, te, nt):
        return (te[i], 0, j)

    def wrow_map(i, j, te, nt):
        return (te[i], j, 0)

    return pl.pallas_call(
        _experts_kernel,
        grid_spec=pltpu.PrefetchScalarGridSpec(
            num_scalar_prefetch=2,
            grid=(max_tiles, dff // tf),
            in_specs=[
                pl.BlockSpec((tm, d), row_map),
                pl.BlockSpec((1, d, tf), wcol_map),
                pl.BlockSpec((1, d, tf), wcol_map),
                pl.BlockSpec((1, tf, d), wrow_map),
            ],
            out_specs=pl.BlockSpec((tm, d), lambda i, j, te, nt: (i, 0)),
            scratch_shapes=[pltpu.VMEM((tm, d), BF16), pltpu.VMEM((tm, d), F32)],
        ),
        out_shape=jax.ShapeDtypeStruct((n_rows, d), F32),
        compiler_params=pltpu.CompilerParams(
            dimension_semantics=("arbitrary", "arbitrary"), vmem_limit_bytes=VMEM_LIMIT),
        name="experts",
    )(tile_expert, n_tiles, xs, w_gate, w_up, w_down)


def _combine_kernel(pos_ref, posn_ref, ys_ref, x_ref, rw_ref, g_ref, o_ref, a_buf, b_buf, sem, *, tb):
    i = pl.program_id(0)
    last = pl.num_programs(0) - 1
    slot = lax.rem(i, 2)

    def gather(idx_ref, s):
        def issue(t, _):
            _row_copy(ys_ref.at[pl.ds(idx_ref[0, 0, t], 1), :], a_buf.at[s, pl.ds(t, 1), :], sem.at[s]).start()
            _row_copy(ys_ref.at[pl.ds(idx_ref[0, 1, t], 1), :], b_buf.at[s, pl.ds(t, 1), :], sem.at[s]).start()
            return 0

        lax.fori_loop(0, tb, issue, 0, unroll=ISSUE_UNROLL)

    @pl.when(i == 0)
    def _():
        gather(pos_ref, slot)

    @pl.when(i < last)
    def _():
        gather(posn_ref, 1 - slot)

    for _ in range(2 * tb):
        _row_copy(ys_ref.at[pl.ds(0, 1), :], a_buf.at[slot, pl.ds(0, 1), :], sem.at[slot]).wait()
    rw = rw_ref[...]
    y = rw[:, 0:1] * a_buf[slot] + rw[:, 1:2] * b_buf[slot]
    o_ref[...] = x_ref[...] + _rms(y, g_ref[...])


def _combine(ys, x3, rw, g, pos, *, tb):
    n, d = x3.shape
    n_tiles = n // tb
    kern = functools.partial(_combine_kernel, tb=tb)
    return pl.pallas_call(
        kern,
        grid=(n_tiles,),
        in_specs=[
            pl.BlockSpec((1, 2, tb), lambda i: (i, 0, 0), memory_space=pltpu.SMEM),
            pl.BlockSpec((1, 2, tb), lambda i: (jnp.minimum(i + 1, n_tiles - 1), 0, 0), memory_space=pltpu.SMEM),
            pl.BlockSpec(memory_space=pl.ANY),
            pl.BlockSpec((tb, d), lambda i: (i, 0)),
            pl.BlockSpec((tb, LANES), lambda i: (i, 0)),
            pl.BlockSpec((1, d), lambda i: (0, 0)),
        ],
        out_specs=pl.BlockSpec((tb, d), lambda i: (i, 0)),
        out_shape=jax.ShapeDtypeStruct((n, d), F32),
        scratch_shapes=[pltpu.VMEM((2, tb, d), F32), pltpu.VMEM((2, tb, d), F32),
                        pltpu.SemaphoreType.DMA((2,))],
        compiler_params=pltpu.CompilerParams(
            dimension_semantics=("arbitrary",), vmem_limit_bytes=VMEM_LIMIT),
        name="combine",
    )(pos, pos, ys, x3, rw, g)


def _pick_tile(n, pref):
    t = min(pref, n)
    while n % t:
        t //= 2
    return t


def kernel(x, norm_g, pool_w, pool_scale, attn_w_in, attn_b_f, attn_w_out, ffn_w_gate, ffn_w_up, ffn_w_down,
           moe_w_router, moe_w_gate, moe_w_up, moe_w_down):
    b, s, d = x.shape
    n = b * s
    nh = d // HEAD_DIM
    tm = _pick_tile(s, 512)

    x = _layer0(x, norm_g[0], pool_w[0].astype(BF16), pool_scale[0],
                ffn_w_gate[0].astype(BF16), ffn_w_up[0].astype(BF16), ffn_w_down[0].astype(BF16), tm=tm)

    w_in = attn_w_in[0]
    w_f = jnp.zeros((d, LANES), F32).at[:, :nh].set(w_in[:, 3 * d:]).astype(BF16)
    b_f = jnp.zeros((1, LANES), F32).at[0, :nh].set(attn_b_f[0])
    w_qvt = jnp.concatenate([w_in[:, :d], w_in[:, 2 * d:3 * d]], axis=1).T.astype(BF16)
    qt, k, vt, c2, qn, kn = _qkv(x, norm_g[1, 0:1], w_in[:, d:2 * d].astype(BF16), w_qvt, w_f, b_f, tm=tm)
    first = _first_live_block(c2, qn, kn, tq=tm, heads_per_step=ATTN_HEADS_PER_STEP)
    o = _attention(qt, k, vt, first, tq=tm)

    w_router = jnp.zeros((d, LANES), F32).at[:, :N_EXPERTS].set(moe_w_router[0])
    x3, ri, rw, counts = _attn_out(o.reshape(n, d), x.reshape(n, d), norm_g[1, 1:3],
                                   attn_w_out[0].astype(BF16), w_router, tm=tm)

    tme = _pick_tile(n, 512)
    counts = counts[0, :N_EXPERTS]
    padded = (counts + tme - 1) // tme * tme
    ends = jnp.cumsum(padded)
    starts = ends - padded
    n_rows = 2 * n + N_EXPERTS * tme
    pos = jnp.stack([starts[ri[:, 0]] + ri[:, 2], starts[ri[:, 1]] + ri[:, 3]], axis=0)
    tb = _pick_tile(n, 256)
    pos = pos.reshape(2, n // tb, tb).transpose(1, 0, 2)
    fill_hi = ends.at[N_EXPERTS - 1].set(n_rows)
    fill = jnp.stack([starts + counts, fill_hi], axis=0).astype(jnp.int32)
    tile_start = jnp.arange(n_rows // tme, dtype=jnp.int32) * tme
    tile_expert = jnp.minimum(jnp.sum(tile_start[:, None] >= ends[None, :], axis=1), N_EXPERTS - 1).astype(jnp.int32)
    n_tiles = (ends[-1:] // tme).astype(jnp.int32)

    xs = _dispatch(x3, norm_g[1, 2:3], pos, fill, n_rows, tb=tb)
    dffe = moe_w_gate.shape[-1]
    tf = dffe // 2 if (dffe // 2) % LANES == 0 else dffe
    ys = _experts(xs, tile_expert, n_tiles, moe_w_gate[0].astype(BF16), moe_w_up[0].astype(BF16),
                  moe_w_down[0].astype(BF16), tm=tme, tf=tf)
    out = _combine(ys, x3, rw, norm_g[1, 3:4], pos, tb=tb)
    return out.reshape(b, s, d)
```

```python
import functools

import jax
import jax.numpy as jnp
from jax import lax
from jax.experimental import pallas as pl
from jax.experimental.pallas import tpu as pltpu

F32 = jnp.float32
BF16 = jnp.bfloat16

RMS_EPS = 1e-6
HEAD_DIM = 64
POOL_WINDOWS = (2, 4, 8, 16)
POOL_HALO = 16
N_EXPERTS = 8
NEG_INF = -1e30
LANES = 128
VMEM_LIMIT = 56 * 1024 * 1024
ISSUE_UNROLL = 8
LOG2E = 1.4426950408889634
ATTN_HEADS_PER_STEP = 4
ROW_CHAINS = 2
META_LANES = 8
STAB_LANE = 6


def _rms(x, g):
    ms = jnp.mean(x * x, axis=-1, keepdims=True)
    return x * lax.rsqrt(ms + RMS_EPS) * g


def _split3(c):
    hi = c.astype(BF16).astype(F32)
    r = c - hi
    mid = r.astype(BF16).astype(F32)
    lo = r - mid
    return hi, mid, lo


def _resident(shape):
    nd = len(shape)
    return pl.BlockSpec(shape, lambda *_: (0,) * nd, pipeline_mode=pl.Buffered(1))


def _layer0_kernel(x_ref, xp_ref, g_ref, pw_ref, ps_ref, wg_ref, wu_ref, wd_ref, o_ref,
                   buf_a, buf_b, *, tm, sub, ff_chunk):
    s = pl.program_id(1)
    d = x_ref.shape[-1]
    gd = d // len(POOL_WINDOWS)
    g = g_ref[...]
    dff = wg_ref.shape[1]
    lo = 8
    top = sub + POOL_HALO + lo
    body = lo + POOL_HALO
    zeros8 = jnp.zeros((lo, d), F32)
    halo = jnp.where(s > 0, _rms(xp_ref[0], g[0:1]), 0.0)

    for idx, r0 in enumerate(range(0, tm, sub)):
        x = x_ref[0, r0:r0 + sub, :]
        h = _rms(x, g[0:1])
        ba = buf_a.at[idx]
        bb = buf_b.at[idx]
        ba[0:lo, :] = zeros8
        bb[0:lo, :] = zeros8
        ba[lo:body, :] = halo
        ba[body:top, :] = h
        halo = h[sub - POOL_HALO:, :]

        def shifted_sum(src, k, c0):
            return src[lo:top, c0:] + src[lo - k:top - k, c0:]

        sums = [ba[body:top, 0:gd] + ba[body - 1:top - 1, 0:gd]]
        bb[lo:top, gd:] = shifted_sum(ba, 1, gd)
        sums.append(bb[body:top, gd:2 * gd] + bb[body - 2:top - 2, gd:2 * gd])
        ba[lo:top, 2 * gd:] = shifted_sum(bb, 2, 2 * gd)
        sums.append(ba[body:top, 2 * gd:3 * gd] + ba[body - 4:top - 4, 2 * gd:3 * gd])
        bb[lo:top, 3 * gd:] = shifted_sum(ba, 4, 3 * gd)
        sums.append(bb[body:top, 3 * gd:] + bb[body - 8:top - 8, 3 * gd:])

        pos = s * tm + r0 + lax.broadcasted_iota(jnp.int32, (sub, 1), 0)
        mixed = []
        for gi, w in enumerate(POOL_WINDOWS):
            count = jnp.minimum(pos + 1, w).astype(F32)
            pooled = sums[gi] / count - h[:, gi * gd:(gi + 1) * gd]
            mixed.append(jnp.dot(pooled.astype(BF16), pw_ref[gi], preferred_element_type=F32))
        y = jnp.concatenate(mixed, axis=-1) * ps_ref[...]
        x1 = x + _rms(y, g[1:2])

        h2 = _rms(x1, g[2:3]).astype(BF16)
        acc = jnp.zeros((sub, d), F32)
        for c0 in range(0, dff, ff_chunk):
            gate = jnp.dot(h2, wg_ref[:, c0:c0 + ff_chunk], preferred_element_type=F32)
            up = jnp.dot(h2, wu_ref[:, c0:c0 + ff_chunk], preferred_element_type=F32)
            act = (gate * jax.nn.sigmoid(gate) * up).astype(BF16)
            acc = acc + jnp.dot(act, wd_ref[c0:c0 + ff_chunk, :], preferred_element_type=F32)
        o_ref[0, r0:r0 + sub, :] = x1 + _rms(acc, g[3:4])


def _layer0(x, g4, pool_w, pool_scale, w_gate, w_up, w_down, *, tm):
    b, s, d = x.shape
    dff = w_gate.shape[1]
    ff_chunk = dff // 2 if (dff // 2) % LANES == 0 else dff
    halo_per_tile = tm // POOL_HALO
    sub = tm // ROW_CHAINS
    kern = functools.partial(_layer0_kernel, tm=tm, sub=sub, ff_chunk=ff_chunk)
    return pl.pallas_call(
        kern,
        grid=(b, s // tm),
        in_specs=[
            pl.BlockSpec((1, tm, d), lambda bi, si: (bi, si, 0)),
            pl.BlockSpec((1, POOL_HALO, d), lambda bi, si: (bi, jnp.maximum(si * halo_per_tile - 1, 0), 0)),
            _resident((4, d)),
            _resident(pool_w.shape),
            _resident((1, d)),
            _resident(w_gate.shape),
            _resident(w_up.shape),
            _resident(w_down.shape),
        ],
        out_specs=pl.BlockSpec((1, tm, d), lambda bi, si: (bi, si, 0)),
        out_shape=jax.ShapeDtypeStruct((b, s, d), F32),
        scratch_shapes=[pltpu.VMEM((ROW_CHAINS, sub + POOL_HALO + 8, d), F32),
                        pltpu.VMEM((ROW_CHAINS, sub + POOL_HALO + 8, d), F32)],
        compiler_params=pltpu.CompilerParams(
            dimension_semantics=("arbitrary", "arbitrary"), vmem_limit_bytes=VMEM_LIMIT),
        name="layer0",
    )(x, x, g4, pool_w, pool_scale.reshape(1, d), w_gate, w_up, w_down)


def _qkv_kernel(x_ref, g_ref, wk_ref, wqvt_ref, wf_ref, bf_ref, tri_ref, hsel_ref,
                qt_ref, k_ref, vt_ref, c2_ref, qn_ref, kn_ref, carry_ref, *, tm, n_heads):
    s = pl.program_id(1)
    d = x_ref.shape[-1]
    nt = (((1,), (1,)), ((), ()))

    @pl.when(s == 0)
    def _():
        carry_ref[...] = jnp.zeros_like(carry_ref)

    sub = tm // ROW_CHAINS
    lane = lax.broadcasted_iota(jnp.int32, (sub, HEAD_DIM), 1)
    row = lax.broadcasted_iota(jnp.int32, (HEAD_DIM, sub), 0)
    aug_vt = jnp.where(row == 0, 1.0, 0.0)
    scale = HEAD_DIM ** -0.5 * LOG2E
    tri = tri_ref[...]
    carry = carry_ref[...]
    qn_max = None
    kn_max = None
    for r0 in range(0, tm, sub):
        rows = pl.ds(r0, sub)
        h = _rms(x_ref[0, rows, :], g_ref[...]).astype(BF16)
        kproj = jnp.dot(h, wk_ref[...], preferred_element_type=F32)
        qvt = lax.dot_general(wqvt_ref[...], h, nt, preferred_element_type=F32)
        z = jnp.dot(h, wf_ref[...], preferred_element_type=F32) + bf_ref[...]
        log_f = jnp.minimum(z, 0.0) - jnp.log(1.0 + jnp.exp(-jnp.abs(z)))

        c = carry
        for piece in _split3(log_f):
            c = c + jnp.dot(tri, piece.astype(BF16), preferred_element_type=F32)
        carry = c[sub - 1:sub, :]
        c = c * LOG2E
        c2_ref[0, rows, :] = c
        c_hi, c_mid, c_lo = _split3(c)
        ct_hi, ct_mid, ct_lo = _split3(c.T)

        q_sq = jnp.square(qvt[:d, :] * scale).reshape(n_heads, HEAD_DIM, sub)
        qn = jnp.max(jnp.sum(q_sq, axis=1), axis=1, keepdims=True)
        k_sq = jnp.dot(jnp.square(kproj).astype(BF16), hsel_ref[...], preferred_element_type=F32)
        kn = jnp.max(k_sq, axis=0, keepdims=True)
        qn_max = qn if qn_max is None else jnp.maximum(qn_max, qn)
        kn_max = kn if kn_max is None else jnp.maximum(kn_max, kn)
        for hd in range(n_heads):
            aug_k = jnp.where(lane == 3, -c_hi[:, hd:hd + 1], jnp.where(
                lane == 4, -c_mid[:, hd:hd + 1], jnp.where(
                    lane == 5, -c_lo[:, hd:hd + 1], jnp.where((lane < 3) | (lane == STAB_LANE), 1.0, 0.0))))
            aug_qt = jnp.where(row == 0, ct_hi[hd:hd + 1, :], jnp.where(
                row == 1, ct_mid[hd:hd + 1, :], jnp.where(
                    row == 2, ct_lo[hd:hd + 1, :], jnp.where(row < 6, 1.0, 0.0))))
            c0 = hd * HEAD_DIM
            k_ref[0, hd, rows, :] = jnp.concatenate([kproj[:, c0:c0 + HEAD_DIM], aug_k], axis=-1).astype(BF16)
            qt_ref[0, hd, :, rows] = jnp.concatenate(
                [qvt[c0:c0 + HEAD_DIM, :] * scale, aug_qt], axis=0).astype(BF16)
            vt_ref[0, hd, :, rows] = jnp.concatenate(
                [qvt[d + c0:d + c0 + HEAD_DIM, :], aug_vt], axis=0).astype(BF16)

    carry_ref[...] = carry
    qn_ref[0, 0] = jnp.broadcast_to(qn_max, qn_ref.shape[2:])
    kn_ref[0, 0] = jnp.broadcast_to(kn_max, kn_ref.shape[2:])


def _qkv(x, g, w_k, w_qvt, w_f, b_f, *, tm):
    b, s, d = x.shape
    n_heads = d // HEAD_DIM
    dk = 2 * HEAD_DIM
    sub = tm // ROW_CHAINS
    tri = jnp.tril(jnp.ones((sub, sub), BF16))
    hsel =(jnp.arange(d)[:, None] // HEAD_DIM == jnp.arange(LANES)[None, :]).astype(BF16)
    kern = functools.partial(_qkv_kernel, tm=tm, n_heads=n_heads)
    row_sds = jax.ShapeDtypeStruct((b, n_heads, s, dk), BF16)
    col_sds = jax.ShapeDtypeStruct((b, n_heads, dk, s), BF16)
    row_spec = pl.BlockSpec((1, n_heads, tm, dk), lambda bi, si: (bi, 0, si, 0))
    col_spec = pl.BlockSpec((1, n_heads, dk, tm), lambda bi, si: (bi, 0, 0, si))
    return pl.pallas_call(
        kern,
        grid=(b, s // tm),
        in_specs=[
            pl.BlockSpec((1, tm, d), lambda bi, si: (bi, si, 0)),
            _resident((1, d)),
            _resident(w_k.shape),
            _resident(w_qvt.shape),
            _resident(w_f.shape),
            _resident(b_f.shape),
            _resident((sub, sub)),
            _resident((d, LANES)),
        ],
        out_specs=[col_spec, row_spec, col_spec,
                   pl.BlockSpec((1, tm, LANES), lambda bi, si: (bi, si, 0)),
                   pl.BlockSpec((1, 1, n_heads, LANES), lambda bi, si: (bi, si, 0, 0)),
                   pl.BlockSpec((1, 1, 8, LANES), lambda bi, si: (bi, si, 0, 0))],
        out_shape=[col_sds, row_sds, col_sds,
                   jax.ShapeDtypeStruct((b, s, LANES), F32),
                   jax.ShapeDtypeStruct((b, s // tm, n_heads, LANES), F32),
                   jax.ShapeDtypeStruct((b, s // tm, 8, LANES), F32)],
        scratch_shapes=[pltpu.VMEM((1, LANES), F32)],
        compiler_params=pltpu.CompilerParams(
            dimension_semantics=("arbitrary", "arbitrary"), vmem_limit_bytes=VMEM_LIMIT),
        name="qkv",
    )(x, g, w_k, w_qvt, w_f, b_f, tri, hsel)


def _attn_kernel(first_ref, stab_ref, qt_ref, k_ref, vt_ref, eye_ref, o_ref, m_ref, acc_ref, s0_ref, s1_ref,
                 *, tq, tk, heads_per_step, fixed_stabiliser):
    qi = pl.program_id(2)
    step = (pl.program_id(0) * pl.num_programs(1) + pl.program_id(1)) * pl.num_programs(2) + qi
    first = first_ref[step]
    m_ref[...] = jnp.full(m_ref.shape, NEG_INF, F32)
    acc_ref[...] = jnp.zeros(acc_ref.shape, F32)

    if fixed_stabiliser:
        row = lax.broadcasted_iota(jnp.int32, qt_ref.shape[2:], 0)
        head0 = (pl.program_id(0) * pl.num_programs(1) + pl.program_id(1)) * heads_per_step
        qts = [jnp.where(row == HEAD_DIM + STAB_LANE, (-stab_ref[head0 + hh]).astype(BF16), qt_ref[0, hh])
               for hh in range(heads_per_step)]
    else:
        qts = [qt_ref[0, hh] for hh in range(heads_per_step)]

    def scores(sub, dst_ref):
        start = pl.multiple_of(sub * tk, tk)
        for hh in range(heads_per_step):
            k = k_ref[0, hh, pl.ds(start, tk), :]
            dst_ref[hh] = jnp.dot(k, qts[hh], preferred_element_type=F32)

    def consume(sub, src_ref, first_key=None):
        start = pl.multiple_of(sub * tk, tk)
        for hh in range(heads_per_step):
            st = src_ref[hh]
            if first_key is not None:
                key = lax.broadcasted_iota(jnp.int32, (tk, tq), 0) + first_key
                qry = lax.broadcasted_iota(jnp.int32, (tk, tq), 1)
                st = jnp.where(key <= qry, st, NEG_INF)
            vt = vt_ref[0, hh, :, pl.ds(start, tk)]
            if fixed_stabiliser:
                acc_ref[hh] += jnp.dot(vt, jnp.exp2(st).astype(BF16), preferred_element_type=F32)
                continue
            m_old = m_ref[hh]
            m_new = jnp.maximum(m_old, jnp.max(st, axis=0, keepdims=True))
            pt = jnp.exp2(st - m_new).astype(BF16)
            alpha = jnp.exp2(m_old - m_new)
            acc_ref[hh] = alpha * acc_ref[hh] + jnp.dot(vt, pt, preferred_element_type=F32)
            m_ref[hh] = m_new

    subs = tq // tk
    assert subs == 2
    scores(2 * first, s0_ref)

    def trip(j, carry):
        scores(2 * j + 1, s1_ref)
        consume(2 * j, s0_ref)
        scores(2 * j + 2, s0_ref)
        consume(2 * j + 1, s1_ref)
        return carry

    lax.fori_loop(first, qi, trip, 0)
    scores(2 * qi + 1, s1_ref)
    consume(2 * qi, s0_ref, first_key=0)
    consume(2 * qi + 1, s1_ref, first_key=tk)
    outs = []
    for hh in range(heads_per_step):
        acc = acc_ref[hh]
        outs.append((acc[:HEAD_DIM, :] / acc[HEAD_DIM:HEAD_DIM + 1, :]).astype(o_ref.dtype))
    o_t = jnp.concatenate(outs, axis=0)
    o_ref[0] = lax.dot_general(eye_ref[...], o_t, (((1,), (1,)), ((), ())),
                               preferred_element_type=F32).astype(o_ref.dtype)


UNDERFLOW_LOG2 = 126.0
SKIP_MARGIN_LOG2 = 8.0
STAB_MAX_SPREAD_LOG2 = 100.0


def _first_live_block(c2, qn, kn, *, tq, heads_per_step):
    b, s, _ = c2.shape
    nh = qn.shape[2]
    nq = s // tq
    qk = jnp.sqrt(jnp.max(qn[:, :, :, 0], axis=1) * jnp.max(kn[:, :, 0, :nh], axis=1)) * 1.03
    gap = 2.0 * qk + UNDERFLOW_LOG2 + SKIP_MARGIN_LOG2
    c_query = c2[:, 0::tq, :nh]
    c_key = c2[:, tq - 1::tq, :nh]
    dead = (c_key[:, None, :, :] - c_query[:, :, None, :]) > gap[:, None, None, :]
    dead = dead & (jnp.arange(nq)[None, None, :, None] < jnp.arange(nq)[None, :, None, None])
    first = jnp.sum(dead, axis=2).astype(jnp.int32)
    first = jnp.min(first.reshape(b, nq, nh // heads_per_step, heads_per_step), axis=-1)
    stab = qk.astype(BF16).astype(F32).reshape(-1)
    stab_ok = jnp.max(2.0 * qk) + SKIP_MARGIN_LOG2 < STAB_MAX_SPREAD_LOG2
    return first.transpose(0, 2, 1).reshape(-1), stab, stab_ok


def _attention(qt, k, vt, first, stab, *, tq, fixed_stabiliser):
    b, nh, s, dk = k.shape
    hps = ATTN_HEADS_PER_STEP
    tk = tq // 2
    kern = functools.partial(_attn_kernel, tq=tq, tk=tk, heads_per_step=hps, fixed_stabiliser=fixed_stabiliser)
    return pl.pallas_call(
        kern,
        grid_spec=pltpu.PrefetchScalarGridSpec(
            num_scalar_prefetch=2,
            grid=(b, nh // hps, s // tq),
            in_specs=[
                pl.BlockSpec((1, hps, dk, tq), lambda bi, hi, qi, f, m: (bi, hi, 0, qi)),
                pl.BlockSpec((1, hps, s, dk), lambda bi, hi, qi, f, m: (bi, hi, 0, 0)),
                pl.BlockSpec((1, hps, dk, s), lambda bi, hi, qi, f, m: (bi, hi, 0, 0)),
                pl.BlockSpec((tq, tq), lambda bi, hi, qi, f, m: (0, 0), pipeline_mode=pl.Buffered(1)),
            ],
            out_specs=pl.BlockSpec((1, tq, hps * HEAD_DIM), lambda bi, hi, qi, f, m: (bi, qi, hi)),
            scratch_shapes=[pltpu.VMEM((hps, 1, tq), F32), pltpu.VMEM((hps, dk, tq), F32),
                            pltpu.VMEM((hps, tk, tq), F32), pltpu.VMEM((hps, tk, tq), F32)],
        ),
        out_shape=jax.ShapeDtypeStruct((b, s, nh * HEAD_DIM), BF16),
        compiler_params=pltpu.CompilerParams(
            dimension_semantics=("arbitrary", "arbitrary", "arbitrary"), vmem_limit_bytes=VMEM_LIMIT),
        name="attention",
    )(first, stab, qt, k, vt, jnp.eye(tq, dtype=BF16))


def _attn_out_kernel(o_ref, x_ref, g_ref, wo_ref, wrh_ref, wrl_ref, ltri_ref, x3_ref, ri_ref, rw_ref, cnt_ref,
                     carry_ref, *, tm, sub):
    i = pl.program_id(0)
    g = g_ref[...]

    @pl.when(i == 0)
    def _():
        carry_ref[...] = jnp.zeros_like(carry_ref)

    carry = carry_ref[...]
    lane = lax.broadcasted_iota(jnp.int32, (sub, LANES), 1)
    for r0 in range(0, tm, sub):
        rows = pl.ds(r0, sub)
        y = jnp.dot(o_ref[rows, :], wo_ref[...], preferred_element_type=F32)
        x3 = x_ref[rows, :] + _rms(y, g[0:1])
        x3_ref[rows, :] = x3
        h = _rms(x3, g[1:2])
        h_hi = h.astype(BF16)
        h_lo = (h - h_hi.astype(F32)).astype(BF16)
        logits = (jnp.dot(h_hi, wrh_ref[...], preferred_element_type=F32)
                  + jnp.dot(h_lo, wrh_ref[...], preferred_element_type=F32)
                  + jnp.dot(h_hi, wrl_ref[...], preferred_element_type=F32))
        logits = jnp.where(lane < N_EXPERTS, logits, -jnp.inf)
        m1 = jnp.max(logits, axis=-1, keepdims=True)
        e1 = jnp.min(jnp.where(logits == m1, lane, LANES), axis=-1, keepdims=True)
        rest = jnp.where(lane == e1, -jnp.inf, logits)
        m2 = jnp.max(rest, axis=-1, keepdims=True)
        e2 = jnp.min(jnp.where(rest == m2, lane, LANES), axis=-1, keepdims=True)
        t = jnp.exp(m2 - m1)
        w1 = 1.0 / (1.0 + t)
        w2 = t / (1.0 + t)

        hot1 = lane == e1
        hot2 = lane == e2
        cnt = jnp.where(hot1 | hot2, 1.0, 0.0)
        before = jnp.dot(ltri_ref[...], cnt.astype(BF16), preferred_element_type=F32) + carry
        r1 = jnp.sum(jnp.where(hot1, before, 0.0), axis=-1, keepdims=True)
        r2 = jnp.sum(jnp.where(hot2, before, 0.0), axis=-1, keepdims=True)
        carry = carry + jnp.sum(cnt, axis=0, keepdims=True)
        ri = jnp.where(lane == 0, e1, jnp.where(lane == 1, e2, jnp.where(
            lane == 2, r1.astype(jnp.int32), jnp.where(lane == 3, r2.astype(jnp.int32), 0))))
        ri_ref[rows, :] = ri[:, :META_LANES]
        rw_ref[rows, :] = jnp.where(lane == 0, w1, jnp.where(lane == 1, w2, 0.0))[:, :META_LANES]

    carry_ref[...] = carry
    cnt_ref[...] = jnp.broadcast_to(carry, cnt_ref.shape).astype(jnp.int32)


def _attn_out(o, x, g2, w_out, w_router_pad, *, tm):
    n, d = x.shape
    sub = tm // 2
    ltri = jnp.tril(jnp.ones((sub, sub), BF16), k=-1)
    wr_hi = w_router_pad.astype(BF16)
    wr_lo = (w_router_pad - wr_hi.astype(F32)).astype(BF16)
    kern = functools.partial(_attn_out_kernel, tm=tm, sub=sub)
    row_spec = pl.BlockSpec((tm, d), lambda i: (i, 0))
    meta_spec = pl.BlockSpec((tm, META_LANES), lambda i: (i, 0))
    return pl.pallas_call(
        kern,
        grid=(n // tm,),
        in_specs=[row_spec, row_spec, _resident((2, d)), _resident(w_out.shape),
                  _resident(wr_hi.shape), _resident(wr_lo.shape), _resident((sub, sub))],
        out_specs=[row_spec, meta_spec, meta_spec, pl.BlockSpec((8, LANES), lambda i: (0, 0))],
        out_shape=[jax.ShapeDtypeStruct((n, d), F32),
                   jax.ShapeDtypeStruct((n, META_LANES), jnp.int32),
                   jax.ShapeDtypeStruct((n, META_LANES), F32),
                   jax.ShapeDtypeStruct((8, LANES), jnp.int32)],
        scratch_shapes=[pltpu.VMEM((1, LANES), F32)],
        compiler_params=pltpu.CompilerParams(
            dimension_semantics=("arbitrary",), vmem_limit_bytes=VMEM_LIMIT),
        name="attn_out",
    )(o, x, g2, w_out, wr_hi, wr_lo, ltri)


def _row_copy(src, dst, sem):
    return pltpu.make_async_copy(src, dst, sem)


def _dispatch_kernel(fill_ref, pos_ref, x_ref, g_ref, xs_ref, h_buf, z_buf, sem, zsem, *, tb):
    i = pl.program_id(0)
    last = pl.num_programs(0) - 1
    slot = lax.rem(i, 2)
    h_buf[slot] = _rms(x_ref[...], g_ref[...])

    def issue(t, _):
        for which in range(2):
            p = pos_ref[0, which, t]
            _row_copy(h_buf.at[slot, pl.ds(t, 1), :], xs_ref.at[pl.ds(p, 1), :], sem.at[slot]).start()
        return 0

    lax.fori_loop(0, tb, issue, 0, unroll=ISSUE_UNROLL)

    def drain(s):
        for _ in range(2 * tb):
            _row_copy(h_buf.at[s, pl.ds(0, 1), :], xs_ref.at[pl.ds(0, 1), :], sem.at[s]).wait()

    @pl.when(i == 0)
    def _():
        z_buf[...] = jnp.zeros_like(z_buf)
        for e in range(N_EXPERTS):
            lo = fill_ref[0, e]
            hi = fill_ref[1, e]

            def fill(r, _):
                _row_copy(z_buf, xs_ref.at[pl.ds(r, 1), :], zsem).start()
                return 0

            lax.fori_loop(lo, hi, fill, 0)

            def fill_done(r, _):
                _row_copy(z_buf, xs_ref.at[pl.ds(0, 1), :], zsem).wait()
                return 0

            lax.fori_loop(lo, hi, fill_done, 0)

    @pl.when(i > 0)
    def _():
        drain(1 - slot)

    @pl.when(i == last)
    def _():
        drain(slot)


def _dispatch(x3, g, pos, fill, n_rows, *, tb):
    n, d = x3.shape
    kern = functools.partial(_dispatch_kernel, tb=tb)
    return pl.pallas_call(
        kern,
        grid_spec=pltpu.PrefetchScalarGridSpec(
            num_scalar_prefetch=1,
            grid=(n // tb,),
            in_specs=[
                pl.BlockSpec((1, 2, tb), lambda i, f: (i, 0, 0), memory_space=pltpu.SMEM),
                pl.BlockSpec((tb, d), lambda i, f: (i, 0)),
                pl.BlockSpec((1, d), lambda i, f: (0, 0)),
            ],
            out_specs=pl.BlockSpec(memory_space=pl.ANY),
            scratch_shapes=[pltpu.VMEM((2, tb, d), F32), pltpu.VMEM((1, d), F32),
                            pltpu.SemaphoreType.DMA((2,)), pltpu.SemaphoreType.DMA(())],
        ),
        out_shape=jax.ShapeDtypeStruct((n_rows, d), F32),
        compiler_params=pltpu.CompilerParams(
            dimension_semantics=("arbitrary",), vmem_limit_bytes=VMEM_LIMIT),
        name="dispatch",
    )(fill, pos, x3, g)


def _experts_kernel(te_ref, nt_ref, xs_ref, wg_ref, wu_ref, wd_ref, ys_ref, xb_ref, acc_ref):
    i = pl.program_id(0)
    j = pl.program_id(1)
    nj = pl.num_programs(1)

    @pl.when(i < nt_ref[0])
    def _():
        @pl.when(j == 0)
        def _():
            xb_ref[...] = xs_ref[...].astype(BF16)
            acc_ref[...] = jnp.zeros_like(acc_ref)

        sub = xb_ref.shape[0] // ROW_CHAINS
        for r0 in range(0, xb_ref.shape[0], sub):
            xb = xb_ref[r0:r0 + sub, :]
            gate = jnp.dot(xb, wg_ref[0], preferred_element_type=F32)
            up = jnp.dot(xb, wu_ref[0], preferred_element_type=F32)
            act = (gate * jax.nn.sigmoid(gate) * up).astype(BF16)
            acc_ref[r0:r0 + sub, :] += jnp.dot(act, wd_ref[0], preferred_element_type=F32)

        @pl.when(j == nj - 1)
        def _():
            ys_ref[...] = acc_ref[...]

    @pl.when((i >= nt_ref[0]) & (j == nj - 1))
    def _():
        ys_ref[...] = jnp.zeros_like(ys_ref)


def _experts(xs, tile_expert, n_tiles, w_gate, w_up, w_down, *, tm, tf):
    n_rows, d = xs.shape
    dff = w_gate.shape[-1]
    max_tiles = n_rows // tm

    def row_map(i, j, te, nt):
        return (jnp.minimum(i, nt[0] - 1), 0)

    def wcol_map(i, j, te, nt):
        return (te[i], 0, j)

    def wrow_map(i, j, te, nt):
        return (te[i], j, 0)

    return pl.pallas_call(
        _experts_kernel,
        grid_spec=pltpu.PrefetchScalarGridSpec(
            num_scalar_prefetch=2,
            grid=(max_tiles, dff // tf),
            in_specs=[
                pl.BlockSpec((tm, d), row_map),
                pl.BlockSpec((1, d, tf), wcol_map),
                pl.BlockSpec((1, d, tf), wcol_map),
                pl.BlockSpec((1, tf, d), wrow_map),
            ],
            out_specs=pl.BlockSpec((tm, d), lambda i, j, te, nt: (i, 0)),
            scratch_shapes=[pltpu.VMEM((tm, d), BF16), pltpu.VMEM((tm, d), F32)],
        ),
        out_shape=jax.ShapeDtypeStruct((n_rows, d), F32),
        compiler_params=pltpu.CompilerParams(
            dimension_semantics=("arbitrary", "arbitrary"), vmem_limit_bytes=VMEM_LIMIT),
        name="experts",
    )(tile_expert, n_tiles, xs, w_gate, w_up, w_down)


def _combine_kernel(pos_ref, posn_ref, ys_ref, x_ref, rw_ref, g_ref, o_ref, a_buf, b_buf, sem, *, tb):
    i = pl.program_id(0)
    last = pl.num_programs(0) - 1
    slot = lax.rem(i, 2)

    def gather(idx_ref, s):
        def issue(t, _):
            _row_copy(ys_ref.at[pl.ds(idx_ref[0, 0, t], 1), :], a_buf.at[s, pl.ds(t, 1), :], sem.at[s]).start()
            _row_copy(ys_ref.at[pl.ds(idx_ref[0, 1, t], 1), :], b_buf.at[s, pl.ds(t, 1), :], sem.at[s]).start()
            return 0

        lax.fori_loop(0, tb, issue, 0, unroll=ISSUE_UNROLL)

    @pl.when(i == 0)
    def _():
        gather(pos_ref, slot)

    @pl.when(i < last)
    def _():
        gather(posn_ref, 1 - slot)

    for _ in range(2 * tb):
        _row_copy(ys_ref.at[pl.ds(0, 1), :], a_buf.at[slot, pl.ds(0, 1), :], sem.at[slot]).wait()
    rw = rw_ref[...]
    y = rw[:, 0:1] * a_buf[slot] + rw[:, 1:2] * b_buf[slot]
    o_ref[...] = x_ref[...] + _rms(y, g_ref[...])


def _combine(ys, x3, rw, g, pos, *, tb):
    n, d = x3.shape
    n_tiles = n // tb
    kern = functools.partial(_combine_kernel, tb=tb)
    return pl.pallas_call(
        kern,
        grid=(n_tiles,),
        in_specs=[
            pl.BlockSpec((1, 2, tb), lambda i: (i, 0, 0), memory_space=pltpu.SMEM),
            pl.BlockSpec((1, 2, tb), lambda i: (jnp.minimum(i + 1, n_tiles - 1), 0, 0), memory_space=pltpu.SMEM),
            pl.BlockSpec(memory_space=pl.ANY),
            pl.BlockSpec((tb, d), lambda i: (i, 0)),
            pl.BlockSpec((tb, META_LANES), lambda i: (i, 0)),
            pl.BlockSpec((1, d), lambda i: (0, 0)),
        ],
        out_specs=pl.BlockSpec((tb, d), lambda i: (i, 0)),
        out_shape=jax.ShapeDtypeStruct((n, d), F32),
        scratch_shapes=[pltpu.VMEM((2, tb, d), F32), pltpu.VMEM((2, tb, d), F32),
                        pltpu.SemaphoreType.DMA((2,))],
        compiler_params=pltpu.CompilerParams(
            dimension_semantics=("arbitrary",), vmem_limit_bytes=VMEM_LIMIT),
        name="combine",
    )(pos, pos, ys, x3, rw, g)


def _pick_tile(n, pref):
    t = min(pref, n)
    while n % t:
        t //= 2
    return t


def kernel(x, norm_g, pool_w, pool_scale, attn_w_in, attn_b_f, attn_w_out, ffn_w_gate, ffn_w_up, ffn_w_down,
           moe_w_router, moe_w_gate, moe_w_up, moe_w_down):
    b, s, d = x.shape
    n = b * s
    nh = d // HEAD_DIM
    tm = _pick_tile(s, 512)

    x = _layer0(x, norm_g[0], pool_w[0].astype(BF16), pool_scale[0],
                ffn_w_gate[0].astype(BF16), ffn_w_up[0].astype(BF16), ffn_w_down[0].astype(BF16), tm=tm)

    w_in = attn_w_in[0]
    w_f = jnp.zeros((d, LANES), F32).at[:, :nh].set(w_in[:, 3 * d:]).astype(BF16)
    b_f = jnp.zeros((1, LANES), F32).at[0, :nh].set(attn_b_f[0])
    w_qvt = jnp.concatenate([w_in[:, :d], w_in[:, 2 * d:3 * d]], axis=1).T.astype(BF16)
    qt, k, vt, c2, qn, kn = _qkv(x, norm_g[1, 0:1], w_in[:, d:2 * d].astype(BF16), w_qvt, w_f, b_f, tm=tm)
    first, stab, stab_ok = _first_live_block(c2, qn, kn, tq=tm, heads_per_step=ATTN_HEADS_PER_STEP)
    o = lax.cond(stab_ok,
                 functools.partial(_attention, tq=tm, fixed_stabiliser=True),
                 functools.partial(_attention, tq=tm, fixed_stabiliser=False),
                 qt, k, vt, first, stab)

    w_router = jnp.zeros((d, LANES), F32).at[:, :N_EXPERTS].set(moe_w_router[0])
    x3, ri, rw, counts = _attn_out(o.reshape(n, d), x.reshape(n, d), norm_g[1, 1:3],
                                   attn_w_out[0].astype(BF16), w_router, tm=tm)

    tme = _pick_tile(n, 512)
    counts = counts[0, :N_EXPERTS]
    padded = (counts + tme - 1) // tme * tme
    ends = jnp.cumsum(padded)
    starts = ends - padded
    n_rows = 2 * n + N_EXPERTS * tme
    pos = jnp.stack([starts[ri[:, 0]] + ri[:, 2], starts[ri[:, 1]] + ri[:, 3]], axis=0)
    tb = _pick_tile(n, 256)
    pos = pos.reshape(2, n // tb, tb).transpose(1, 0, 2)
    fill_hi = ends.at[N_EXPERTS - 1].set(n_rows)
    fill = jnp.stack([starts + counts, fill_hi], axis=0).astype(jnp.int32)
    tile_start = jnp.arange(n_rows // tme, dtype=jnp.int32) * tme
    tile_expert = jnp.minimum(jnp.sum(tile_start[:, None] >= ends[None, :], axis=1), N_EXPERTS - 1).astype(jnp.int32)
    n_tiles = (ends[-1:] // tme).astype(jnp.int32)

    xs = _dispatch(x3, norm_g[1, 2:3], pos, fill, n_rows, tb=tb)
    dffe = moe_w_gate.shape[-1]
    tf = dffe // 2 if (dffe // 2) % LANES == 0 else dffe
    ys = _experts(xs, tile_expert, n_tiles, moe_w_gate[0].astype(BF16), moe_w_up[0].astype(BF16),
                  moe_w_down[0].astype(BF16), tm=tme, tf=tf)
    out = _combine(ys, x3, rw, norm_g[1, 3:4], pos, tb=tb)
    return out.reshape(b, s, d)
```

```python
import functools

import jax
import jax.numpy as jnp
from jax import lax
from jax.experimental import pallas as pl
from jax.experimental.pallas import tpu as pltpu

F32 = jnp.float32
BF16 = jnp.bfloat16

RMS_EPS = 1e-6
HEAD_DIM = 64
POOL_WINDOWS = (2, 4, 8, 16)
POOL_HALO = 16
N_EXPERTS = 8
NEG_INF = -1e30
LANES = 128
VMEM_LIMIT = 56 * 1024 * 1024
ISSUE_UNROLL = 8
LOG2E = 1.4426950408889634
ATTN_HEADS_PER_STEP = 4
ROW_CHAINS = 2
LAYER0_CHAINS = 2
META_LANES = 8
STAB_LANE = 6


def _rms(x, g):
    ms = jnp.mean(x * x, axis=-1, keepdims=True)
    return x * lax.rsqrt(ms + RMS_EPS) * g


def _split3(c):
    hi = c.astype(BF16).astype(F32)
    r = c - hi
    mid = r.astype(BF16).astype(F32)
    lo = r - mid
    return hi, mid, lo


def _resident(shape):
    nd = len(shape)
    return pl.BlockSpec(shape, lambda *_: (0,) * nd, pipeline_mode=pl.Buffered(1))


def _layer0_kernel(x_ref, xp_ref, g_ref, pw_ref, ps_ref, wg_ref, wu_ref, wd_ref, o_ref,
                   buf_a, buf_b, *, tm, sub, ff_chunk):
    s = pl.program_id(1)
    d = x_ref.shape[-1]
    gd = d // len(POOL_WINDOWS)
    g = g_ref[...]
    dff = wg_ref.shape[1]
    lo = 8
    top = sub + POOL_HALO + lo
    body = lo + POOL_HALO
    zeros8 = jnp.zeros((lo, d), F32)
    halo = jnp.where(s > 0, _rms(xp_ref[0], g[0:1]), 0.0)

    for idx, r0 in enumerate(range(0, tm, sub)):
        x = x_ref[0, r0:r0 + sub, :]
        h = _rms(x, g[0:1])
        ba = buf_a.at[idx]
        bb = buf_b.at[idx]
        ba[0:lo, :] = zeros8
        bb[0:lo, :] = zeros8
        ba[lo:body, :] = halo
        ba[body:top, :] = h
        halo = h[sub - POOL_HALO:, :]

        def shifted_sum(src, k, c0):
            return src[lo:top, c0:] + src[lo - k:top - k, c0:]

        sums = [ba[body:top, 0:gd] + ba[body - 1:top - 1, 0:gd]]
        bb[lo:top, gd:] = shifted_sum(ba, 1, gd)
        sums.append(bb[body:top, gd:2 * gd] + bb[body - 2:top - 2, gd:2 * gd])
        ba[lo:top, 2 * gd:] = shifted_sum(bb, 2, 2 * gd)
        sums.append(ba[body:top, 2 * gd:3 * gd] + ba[body - 4:top - 4, 2 * gd:3 * gd])
        bb[lo:top, 3 * gd:] = shifted_sum(ba, 4, 3 * gd)
        sums.append(bb[body:top, 3 * gd:] + bb[body - 8:top - 8, 3 * gd:])

        pos = s * tm + r0 + lax.broadcasted_iota(jnp.int32, (sub, 1), 0)
        mixed = []
        for gi, w in enumerate(POOL_WINDOWS):
            count = jnp.minimum(pos + 1, w).astype(F32)
            pooled = sums[gi] / count - h[:, gi * gd:(gi + 1) * gd]
            mixed.append(jnp.dot(pooled.astype(BF16), pw_ref[gi], preferred_element_type=F32))
        y = jnp.concatenate(mixed, axis=-1) * ps_ref[...]
        x1 = x + _rms(y, g[1:2])

        h2 = _rms(x1, g[2:3]).astype(BF16)
        acc = jnp.zeros((sub, d), F32)
        for c0 in range(0, dff, ff_chunk):
            gate = jnp.dot(h2, wg_ref[:, c0:c0 + ff_chunk], preferred_element_type=F32)
            up = jnp.dot(h2, wu_ref[:, c0:c0 + ff_chunk], preferred_element_type=F32)
            act = (gate * jax.nn.sigmoid(gate) * up).astype(BF16)
            acc = acc + jnp.dot(act, wd_ref[c0:c0 + ff_chunk, :], preferred_element_type=F32)
        o_ref[0, r0:r0 + sub, :] = x1 + _rms(acc, g[3:4])


def _layer0(x, g4, pool_w, pool_scale, w_gate, w_up, w_down, *, tm):
    b, s, d = x.shape
    dff = w_gate.shape[1]
    ff_chunk = dff
    halo_per_tile = tm // POOL_HALO
    sub = tm // LAYER0_CHAINS
    kern = functools.partial(_layer0_kernel, tm=tm, sub=sub, ff_chunk=ff_chunk)
    return pl.pallas_call(
        kern,
        grid=(b, s // tm),
        in_specs=[
            pl.BlockSpec((1, tm, d), lambda bi, si: (bi, si, 0)),
            pl.BlockSpec((1, POOL_HALO, d), lambda bi, si: (bi, jnp.maximum(si * halo_per_tile - 1, 0), 0)),
            _resident((4, d)),
            _resident(pool_w.shape),
            _resident((1, d)),
            _resident(w_gate.shape),
            _resident(w_up.shape),
            _resident(w_down.shape),
        ],
        out_specs=pl.BlockSpec((1, tm, d), lambda bi, si: (bi, si, 0)),
        out_shape=jax.ShapeDtypeStruct((b, s, d), F32),
        scratch_shapes=[pltpu.VMEM((LAYER0_CHAINS, sub + POOL_HALO + 8, d), F32),
                        pltpu.VMEM((LAYER0_CHAINS, sub + POOL_HALO + 8, d), F32)],
        compiler_params=pltpu.CompilerParams(
            dimension_semantics=("arbitrary", "arbitrary"), vmem_limit_bytes=VMEM_LIMIT),
        name="layer0",
    )(x, x, g4, pool_w, pool_scale.reshape(1, d), w_gate, w_up, w_down)


def _qkv_kernel(x_ref, g_ref, wk_ref, wqvt_ref, wf_ref, bf_ref, tri_ref, hsel_ref,
                qt_ref, k_ref, vt_ref, c2_ref, qn_ref, kn_ref, carry_ref, *, tm, n_heads):
    s = pl.program_id(1)
    d = x_ref.shape[-1]
    nt = (((1,), (1,)), ((), ()))

    @pl.when(s == 0)
    def _():
        carry_ref[...] = jnp.zeros_like(carry_ref)

    sub = tm // ROW_CHAINS
    lane = lax.broadcasted_iota(jnp.int32, (sub, HEAD_DIM), 1)
    row = lax.broadcasted_iota(jnp.int32, (HEAD_DIM, sub), 0)
    aug_vt = jnp.where(row == 0, 1.0, 0.0)
    scale = HEAD_DIM ** -0.5 * LOG2E
    tri = tri_ref[...]
    carry = carry_ref[...]
    qn_max = None
    kn_max = None
    for r0 in range(0, tm, sub):
        rows = pl.ds(r0, sub)
        h = _rms(x_ref[0, rows, :], g_ref[...]).astype(BF16)
        kproj = jnp.dot(h, wk_ref[...], preferred_element_type=F32)
        qvt = lax.dot_general(wqvt_ref[...], h, nt, preferred_element_type=F32)
        z = jnp.dot(h, wf_ref[...], preferred_element_type=F32) + bf_ref[...]
        log_f = jnp.minimum(z, 0.0) - jnp.log(1.0 + jnp.exp(-jnp.abs(z)))

        c = carry
        for piece in _split3(log_f):
            c = c + jnp.dot(tri, piece.astype(BF16), preferred_element_type=F32)
        carry = c[sub - 1:sub, :]
        c = c * LOG2E
        c2_ref[0, rows, :] = c
        c_hi, c_mid, c_lo = _split3(c)
        ct_hi, ct_mid, ct_lo = _split3(c.T)

        q_sq = jnp.square(qvt[:d, :] * scale).reshape(n_heads, HEAD_DIM, sub)
        qn = jnp.max(jnp.sum(q_sq, axis=1), axis=1, keepdims=True)
        k_sq = jnp.dot(jnp.square(kproj).astype(BF16), hsel_ref[...], preferred_element_type=F32)
        kn = jnp.max(k_sq, axis=0, keepdims=True)
        qn_max = qn if qn_max is None else jnp.maximum(qn_max, qn)
        kn_max = kn if kn_max is None else jnp.maximum(kn_max, kn)
        for hd in range(n_heads):
            aug_k = jnp.where(lane == 3, -c_hi[:, hd:hd + 1], jnp.where(
                lane == 4, -c_mid[:, hd:hd + 1], jnp.where(
                    lane == 5, -c_lo[:, hd:hd + 1], jnp.where((lane < 3) | (lane == STAB_LANE), 1.0, 0.0))))
            aug_qt = jnp.where(row == 0, ct_hi[hd:hd + 1, :], jnp.where(
                row == 1, ct_mid[hd:hd + 1, :], jnp.where(
                    row == 2, ct_lo[hd:hd + 1, :], jnp.where(row < 6, 1.0, 0.0))))
            c0 = hd * HEAD_DIM
            k_ref[0, hd, rows, :] = jnp.concatenate([kproj[:, c0:c0 + HEAD_DIM], aug_k], axis=-1).astype(BF16)
            qt_ref[0, hd, :, rows] = jnp.concatenate(
                [qvt[c0:c0 + HEAD_DIM, :] * scale, aug_qt], axis=0).astype(BF16)
            vt_ref[0, hd, :, rows] = jnp.concatenate(
                [qvt[d + c0:d + c0 + HEAD_DIM, :], aug_vt], axis=0).astype(BF16)

    carry_ref[...] = carry
    qn_ref[0, 0] = jnp.broadcast_to(qn_max, qn_ref.shape[2:])
    kn_ref[0, 0] = jnp.broadcast_to(kn_max, kn_ref.shape[2:])


def _qkv(x, g, w_k, w_qvt, w_f, b_f, *, tm):
    b, s, d = x.shape
    n_heads = d // HEAD_DIM
    dk = 2 * HEAD_DIM
    sub = tm // ROW_CHAINS
    tri = jnp.tril(jnp.ones((sub, sub), BF16))
    hsel =(jnp.arange(d)[:, None] // HEAD_DIM == jnp.arange(LANES)[None, :]).astype(BF16)
    kern = functools.partial(_qkv_kernel, tm=tm, n_heads=n_heads)
    row_sds = jax.ShapeDtypeStruct((b, n_heads, s, dk), BF16)
    col_sds = jax.ShapeDtypeStruct((b, n_heads, dk, s), BF16)
    row_spec = pl.BlockSpec((1, n_heads, tm, dk), lambda bi, si: (bi, 0, si, 0))
    col_spec = pl.BlockSpec((1, n_heads, dk, tm), lambda bi, si: (bi, 0, 0, si))
    return pl.pallas_call(
        kern,
        grid=(b, s // tm),
        in_specs=[
            pl.BlockSpec((1, tm, d), lambda bi, si: (bi, si, 0)),
            _resident((1, d)),
            _resident(w_k.shape),
            _resident(w_qvt.shape),
            _resident(w_f.shape),
            _resident(b_f.shape),
            _resident((sub, sub)),
            _resident((d, LANES)),
        ],
        out_specs=[col_spec, row_spec, col_spec,
                   pl.BlockSpec((1, tm, LANES), lambda bi, si: (bi, si, 0)),
                   pl.BlockSpec((1, 1, n_heads, LANES), lambda bi, si: (bi, si, 0, 0)),
                   pl.BlockSpec((1, 1, 8, LANES), lambda bi, si: (bi, si, 0, 0))],
        out_shape=[col_sds, row_sds, col_sds,
                   jax.ShapeDtypeStruct((b, s, LANES), F32),
                   jax.ShapeDtypeStruct((b, s // tm, n_heads, LANES), F32),
                   jax.ShapeDtypeStruct((b, s // tm, 8, LANES), F32)],
        scratch_shapes=[pltpu.VMEM((1, LANES), F32)],
        compiler_params=pltpu.CompilerParams(
            dimension_semantics=("arbitrary", "arbitrary"), vmem_limit_bytes=VMEM_LIMIT),
        name="qkv",
    )(x, g, w_k, w_qvt, w_f, b_f, tri, hsel)


def _attn_kernel(first_ref, stab_ref, qt_ref, k_ref, vt_ref, eye_ref, o_ref, m_ref, acc_ref, s0_ref, s1_ref,
                 *, tq, tk, heads_per_step, fixed_stabiliser):
    qi = pl.program_id(2)
    step = (pl.program_id(0) * pl.num_programs(1) + pl.program_id(1)) * pl.num_programs(2) + qi
    first = first_ref[step]
    m_ref[...] = jnp.full(m_ref.shape, NEG_INF, F32)
    acc_ref[...] = jnp.zeros(acc_ref.shape, F32)

    if fixed_stabiliser:
        row = lax.broadcasted_iota(jnp.int32, qt_ref.shape[2:], 0)
        head0 = (pl.program_id(0) * pl.num_programs(1) + pl.program_id(1)) * heads_per_step
        qts = [jnp.where(row == HEAD_DIM + STAB_LANE, (-stab_ref[head0 + hh]).astype(BF16), qt_ref[0, hh])
               for hh in range(heads_per_step)]
    else:
        qts = [qt_ref[0, hh] for hh in range(heads_per_step)]

    def scores(sub, dst_ref):
        start = pl.multiple_of(sub * tk, tk)
        for hh in range(heads_per_step):
            k = k_ref[0, hh, pl.ds(start, tk), :]
            dst_ref[hh] = jnp.dot(k, qts[hh], preferred_element_type=F32)

    def consume(sub, src_ref, first_key=None):
        start = pl.multiple_of(sub * tk, tk)
        for hh in range(heads_per_step):
            st = src_ref[hh]
            if first_key is not None:
                key = lax.broadcasted_iota(jnp.int32, (tk, tq), 0) + first_key
                qry = lax.broadcasted_iota(jnp.int32, (tk, tq), 1)
                st = jnp.where(key <= qry, st, NEG_INF)
            vt = vt_ref[0, hh, :, pl.ds(start, tk)]
            if fixed_stabiliser:
                acc_ref[hh] += jnp.dot(vt, jnp.exp2(st).astype(BF16), preferred_element_type=F32)
                continue
            m_old = m_ref[hh]
            m_new = jnp.maximum(m_old, jnp.max(st, axis=0, keepdims=True))
            pt = jnp.exp2(st - m_new).astype(BF16)
            alpha = jnp.exp2(m_old - m_new)
            acc_ref[hh] = alpha * acc_ref[hh] + jnp.dot(vt, pt, preferred_element_type=F32)
            m_ref[hh] = m_new

    subs = tq // tk
    assert subs == 2
    scores(2 * first, s0_ref)

    def trip(j, carry):
        scores(2 * j + 1, s1_ref)
        consume(2 * j, s0_ref)
        scores(2 * j + 2, s0_ref)
        consume(2 * j + 1, s1_ref)
        return carry

    n_blocks = qi - first

    def double_trip(p, carry):
        trip(first + 2 * p, carry)
        return trip(first + 2 * p + 1, carry)

    lax.fori_loop(0, n_blocks // 2, double_trip, 0)

    @pl.when(n_blocks % 2 == 1)
    def _():
        trip(qi - 1, 0)

    scores(2 * qi + 1, s1_ref)
    consume(2 * qi, s0_ref, first_key=0)
    consume(2 * qi + 1, s1_ref, first_key=tk)
    outs = []
    for hh in range(heads_per_step):
        acc = acc_ref[hh]
        outs.append((acc[:HEAD_DIM, :] / acc[HEAD_DIM:HEAD_DIM + 1, :]).astype(o_ref.dtype))
    o_t = jnp.concatenate(outs, axis=0)
    o_ref[0] = lax.dot_general(eye_ref[...], o_t, (((1,), (1,)), ((), ())),
                               preferred_element_type=F32).astype(o_ref.dtype)


UNDERFLOW_LOG2 = 126.0
SKIP_MARGIN_LOG2 = 8.0
STAB_MAX_SPREAD_LOG2 = 100.0


def _first_live_block(c2, qn, kn, *, tq, heads_per_step):
    b, s, _ = c2.shape
    nh = qn.shape[2]
    nq = s // tq
    qk = jnp.sqrt(jnp.max(qn[:, :, :, 0], axis=1) * jnp.max(kn[:, :, 0, :nh], axis=1)) * 1.03
    gap = 2.0 * qk + UNDERFLOW_LOG2 + SKIP_MARGIN_LOG2
    c_query = c2[:, 0::tq, :nh]
    c_key = c2[:, tq - 1::tq, :nh]
    dead = (c_key[:, None, :, :] - c_query[:, :, None, :]) > gap[:, None, None, :]
    dead = dead & (jnp.arange(nq)[None, None, :, None] < jnp.arange(nq)[None, :, None, None])
    first = jnp.sum(dead, axis=2).astype(jnp.int32)
    first = jnp.min(first.reshape(b, nq, nh // heads_per_step, heads_per_step), axis=-1)
    stab = qk.astype(BF16).astype(F32).reshape(-1)
    stab_ok = jnp.max(2.0 * qk) + SKIP_MARGIN_LOG2 < STAB_MAX_SPREAD_LOG2
    return first.transpose(0, 2, 1).reshape(-1), stab, stab_ok


def _attention(qt, k, vt, first, stab, *, tq, fixed_stabiliser):
    b, nh, s, dk = k.shape
    hps = ATTN_HEADS_PER_STEP
    tk = tq // 2
    kern = functools.partial(_attn_kernel, tq=tq, tk=tk, heads_per_step=hps, fixed_stabiliser=fixed_stabiliser)
    return pl.pallas_call(
        kern,
        grid_spec=pltpu.PrefetchScalarGridSpec(
            num_scalar_prefetch=2,
            grid=(b, nh // hps, s // tq),
            in_specs=[
                pl.BlockSpec((1, hps, dk, tq), lambda bi, hi, qi, f, m: (bi, hi, 0, qi)),
                pl.BlockSpec((1, hps, s, dk), lambda bi, hi, qi, f, m: (bi, hi, 0, 0)),
                pl.BlockSpec((1, hps, dk, s), lambda bi, hi, qi, f, m: (bi, hi, 0, 0)),
                pl.BlockSpec((tq, tq), lambda bi, hi, qi, f, m: (0, 0), pipeline_mode=pl.Buffered(1)),
            ],
            out_specs=pl.BlockSpec((1, tq, hps * HEAD_DIM), lambda bi, hi, qi, f, m: (bi, qi, hi)),
            scratch_shapes=[pltpu.VMEM((hps, 1, tq), F32), pltpu.VMEM((hps, dk, tq), F32),
                            pltpu.VMEM((hps, tk, tq), F32), pltpu.VMEM((hps, tk, tq), F32)],
        ),
        out_shape=jax.ShapeDtypeStruct((b, s, nh * HEAD_DIM), BF16),
        compiler_params=pltpu.CompilerParams(
            dimension_semantics=("arbitrary", "arbitrary", "arbitrary"), vmem_limit_bytes=VMEM_LIMIT),
        name="attention",
    )(first, stab, qt, k, vt, jnp.eye(tq, dtype=BF16))


def _attn_out_kernel(o_ref, x_ref, g_ref, wo_ref, wrh_ref, wrl_ref, ltri_ref, x3_ref, ri_ref, rw_ref, cnt_ref,
                     carry_ref, *, tm, sub):
    i = pl.program_id(0)
    g = g_ref[...]

    @pl.when(i == 0)
    def _():
        carry_ref[...] = jnp.zeros_like(carry_ref)

    carry = carry_ref[...]
    lane = lax.broadcasted_iota(jnp.int32, (sub, LANES), 1)
    for r0 in range(0, tm, sub):
        rows = pl.ds(r0, sub)
        y = jnp.dot(o_ref[rows, :], wo_ref[...], preferred_element_type=F32)
        x3 = x_ref[rows, :] + _rms(y, g[0:1])
        x3_ref[rows, :] = x3
        h = _rms(x3, g[1:2])
        h_hi = h.astype(BF16)
        h_lo = (h - h_hi.astype(F32)).astype(BF16)
        logits = (jnp.dot(h_hi, wrh_ref[...], preferred_element_type=F32)
                  + jnp.dot(h_lo, wrh_ref[...], preferred_element_type=F32)
                  + jnp.dot(h_hi, wrl_ref[...], preferred_element_type=F32))
        logits = jnp.where(lane < N_EXPERTS, logits, -jnp.inf)
        m1 = jnp.max(logits, axis=-1, keepdims=True)
        e1 = jnp.min(jnp.where(logits == m1, lane, LANES), axis=-1, keepdims=True)
        rest = jnp.where(lane == e1, -jnp.inf, logits)
        m2 = jnp.max(rest, axis=-1, keepdims=True)
        e2 = jnp.min(jnp.where(rest == m2, lane, LANES), axis=-1, keepdims=True)
        t = jnp.exp(m2 - m1)
        w1 = 1.0 / (1.0 + t)
        w2 = t / (1.0 + t)

        hot1 = lane == e1
        hot2 = lane == e2
        cnt = jnp.where(hot1 | hot2, 1.0, 0.0)
        before = jnp.dot(ltri_ref[...], cnt.astype(BF16), preferred_element_type=F32) + carry
        r1 = jnp.sum(jnp.where(hot1, before, 0.0), axis=-1, keepdims=True)
        r2 = jnp.sum(jnp.where(hot2, before, 0.0), axis=-1, keepdims=True)
        carry = carry + jnp.sum(cnt, axis=0, keepdims=True)
        ri = jnp.where(lane == 0, e1, jnp.where(lane == 1, e2, jnp.where(
            lane == 2, r1.astype(jnp.int32), jnp.where(lane == 3, r2.astype(jnp.int32), 0))))
        ri_ref[rows, :] = ri[:, :META_LANES]
        rw_ref[rows, :] = jnp.where(lane == 0, w1, jnp.where(lane == 1, w2, 0.0))[:, :META_LANES]

    carry_ref[...] = carry
    cnt_ref[...] = jnp.broadcast_to(carry, cnt_ref.shape).astype(jnp.int32)


def _attn_out(o, x, g2, w_out, w_router_pad, *, tm):
    n, d = x.shape
    sub = tm // 2
    ltri = jnp.tril(jnp.ones((sub, sub), BF16), k=-1)
    wr_hi = w_router_pad.astype(BF16)
    wr_lo = (w_router_pad - wr_hi.astype(F32)).astype(BF16)
    kern = functools.partial(_attn_out_kernel, tm=tm, sub=sub)
    row_spec = pl.BlockSpec((tm, d), lambda i: (i, 0))
    meta_spec = pl.BlockSpec((tm, META_LANES), lambda i: (i, 0))
    return pl.pallas_call(
        kern,
        grid=(n // tm,),
        in_specs=[row_spec, row_spec, _resident((2, d)), _resident(w_out.shape),
                  _resident(wr_hi.shape), _resident(wr_lo.shape), _resident((sub, sub))],
        out_specs=[row_spec, meta_spec, meta_spec, pl.BlockSpec((8, LANES), lambda i: (0, 0))],
        out_shape=[jax.ShapeDtypeStruct((n, d), F32),
                   jax.ShapeDtypeStruct((n, META_LANES), jnp.int32),
                   jax.ShapeDtypeStruct((n, META_LANES), F32),
                   jax.ShapeDtypeStruct((8, LANES), jnp.int32)],
        scratch_shapes=[pltpu.VMEM((1, LANES), F32)],
        compiler_params=pltpu.CompilerParams(
            dimension_semantics=("arbitrary",), vmem_limit_bytes=VMEM_LIMIT),
        name="attn_out",
    )(o, x, g2, w_out, wr_hi, wr_lo, ltri)


def _row_copy(src, dst, sem):
    return pltpu.make_async_copy(src, dst, sem)


def _dispatch_kernel(fill_ref, pos_ref, x_ref, g_ref, xs_ref, h_buf, z_buf, sem, zsem, *, tb):
    i = pl.program_id(0)
    last = pl.num_programs(0) - 1
    slot = lax.rem(i, 2)
    h_buf[slot] = _rms(x_ref[...], g_ref[...])

    def issue(t, _):
        for which in range(2):
            p = pos_ref[0, which, t]
            _row_copy(h_buf.at[slot, pl.ds(t, 1), :], xs_ref.at[pl.ds(p, 1), :], sem.at[slot]).start()
        return 0

    lax.fori_loop(0, tb, issue, 0, unroll=ISSUE_UNROLL)

    def drain(s):
        for _ in range(2 * tb):
            _row_copy(h_buf.at[s, pl.ds(0, 1), :], xs_ref.at[pl.ds(0, 1), :], sem.at[s]).wait()

    @pl.when(i == 0)
    def _():
        z_buf[...] = jnp.zeros_like(z_buf)
        for e in range(N_EXPERTS):
            lo = fill_ref[0, e]
            hi = fill_ref[1, e]

            def fill(r, _):
                _row_copy(z_buf, xs_ref.at[pl.ds(r, 1), :], zsem).start()
                return 0

            lax.fori_loop(lo, hi, fill, 0)

            def fill_done(r, _):
                _row_copy(z_buf, xs_ref.at[pl.ds(0, 1), :], zsem).wait()
                return 0

            lax.fori_loop(lo, hi, fill_done, 0)

    @pl.when(i > 0)
    def _():
        drain(1 - slot)

    @pl.when(i == last)
    def _():
        drain(slot)


def _dispatch(x3, g, pos, fill, n_rows, *, tb):
    n, d = x3.shape
    kern = functools.partial(_dispatch_kernel, tb=tb)
    return pl.pallas_call(
        kern,
        grid_spec=pltpu.PrefetchScalarGridSpec(
            num_scalar_prefetch=1,
            grid=(n // tb,),
            in_specs=[
                pl.BlockSpec((1, 2, tb), lambda i, f: (i, 0, 0), memory_space=pltpu.SMEM),
                pl.BlockSpec((tb, d), lambda i, f: (i, 0)),
                pl.BlockSpec((1, d), lambda i, f: (0, 0)),
            ],
            out_specs=pl.BlockSpec(memory_space=pl.ANY),
            scratch_shapes=[pltpu.VMEM((2, tb, d), F32), pltpu.VMEM((1, d), F32),
                            pltpu.SemaphoreType.DMA((2,)), pltpu.SemaphoreType.DMA(())],
        ),
        out_shape=jax.ShapeDtypeStruct((n_rows, d), F32),
        compiler_params=pltpu.CompilerParams(
            dimension_semantics=("arbitrary",), vmem_limit_bytes=VMEM_LIMIT),
        name="dispatch",
    )(fill, pos, x3, g)


def _experts_kernel(te_ref, nt_ref, xs_ref, wg_ref, wu_ref, wd_ref, ys_ref, xb_ref, acc_ref, *, nj):
    i = pl.program_id(0)
    j = pl.program_id(1)
    live = i < nt_ref[0]
    tm = xb_ref.shape[0]
    sub = tm // ROW_CHAINS

    def partial_out(xb):
        gate = jnp.dot(xb, wg_ref[0], preferred_element_type=F32)
        up = jnp.dot(xb, wu_ref[0], preferred_element_type=F32)
        act = (gate * jax.nn.sigmoid(gate) * up).astype(BF16)
        return jnp.dot(act, wd_ref[0], preferred_element_type=F32)

    @pl.when(live & (j == 0))
    def _():
        for r0 in range(0, tm, sub):
            xb = xs_ref[r0:r0 + sub, :].astype(BF16)
            if nj > 1:
                xb_ref[r0:r0 + sub, :] = xb
                acc_ref[r0:r0 + sub, :] = partial_out(xb)
            else:
                ys_ref[r0:r0 + sub, :] = partial_out(xb)

    if nj > 2:
        @pl.when(live & (j > 0) & (j < nj - 1))
        def _():
            for r0 in range(0, tm, sub):
                acc_ref[r0:r0 + sub, :] += partial_out(xb_ref[r0:r0 + sub, :])

    if nj > 1:
        @pl.when(live & (j == nj - 1))
        def _():
            for r0 in range(0, tm, sub):
                ys_ref[r0:r0 + sub, :] = acc_ref[r0:r0 + sub, :] + partial_out(xb_ref[r0:r0 + sub, :])

    @pl.when(jnp.logical_not(live) & (j == nj - 1))
    def _():
        ys_ref[...] = jnp.zeros_like(ys_ref)


def _experts(xs, tile_expert, n_tiles, w_gate, w_up, w_down, *, tm, tf):
    n_rows, d = xs.shape
    dff = w_gate.shape[-1]
    max_tiles = n_rows // tm

    def row_map(i, j, te, nt):
        return (jnp.minimum(i, nt[0] - 1), 0)

    def wcol_map(i, j, te, nt):
        return (te[i], 0, j)

    def wrow_map(i, j, te, nt):
        return (te[i], j, 0)

    return pl.pallas_call(
        functools.partial(_experts_kernel, nj=dff // tf),
        grid_spec=pltpu.PrefetchScalarGridSpec(
            num_scalar_prefetch=2,
            grid=(max_tiles, dff // tf),
            in_specs=[
                pl.BlockSpec((tm, d), row_map),
                pl.BlockSpec((1, d, tf), wcol_map),
                pl.BlockSpec((1, d, tf), wcol_map),
                pl.BlockSpec((1, tf, d), wrow_map),
            ],
            out_specs=pl.BlockSpec((tm, d), lambda i, j, te, nt: (i, 0)),
            scratch_shapes=[pltpu.VMEM((tm, d), BF16), pltpu.VMEM((tm, d), F32)],
        ),
        out_shape=jax.ShapeDtypeStruct((n_rows, d), F32),
        compiler_params=pltpu.CompilerParams(
            dimension_semantics=("arbitrary", "arbitrary"), vmem_limit_bytes=VMEM_LIMIT),
        name="experts",
    )(tile_expert, n_tiles, xs, w_gate, w_up, w_down)


def _combine_kernel(pos_ref, posn_ref, ys_ref, x_ref, rw_ref, g_ref, o_ref, a_buf, b_buf, sem, *, tb):
    i = pl.program_id(0)
    last = pl.num_programs(0) - 1
    slot = lax.rem(i, 2)

    def gather(idx_ref, s):
        def issue(t, _):
            _row_copy(ys_ref.at[pl.ds(idx_ref[0, 0, t], 1), :], a_buf.at[s, pl.ds(t, 1), :], sem.at[s]).start()
            _row_copy(ys_ref.at[pl.ds(idx_ref[0, 1, t], 1), :], b_buf.at[s, pl.ds(t, 1), :], sem.at[s]).start()
            return 0

        lax.fori_loop(0, tb, issue, 0, unroll=ISSUE_UNROLL)

    @pl.when(i == 0)
    def _():
        gather(pos_ref, slot)

    @pl.when(i < last)
    def _():
        gather(posn_ref, 1 - slot)

    for _ in range(2 * tb):
        _row_copy(ys_ref.at[pl.ds(0, 1), :], a_buf.at[slot, pl.ds(0, 1), :], sem.at[slot]).wait()
    rw = rw_ref[...]
    y = rw[:, 0:1] * a_buf[slot] + rw[:, 1:2] * b_buf[slot]
    o_ref[...] = x_ref[...] + _rms(y, g_ref[...])


def _combine(ys, x3, rw, g, pos, *, tb):
    n, d = x3.shape
    n_tiles = n // tb
    kern = functools.partial(_combine_kernel, tb=tb)
    return pl.pallas_call(
        kern,
        grid=(n_tiles,),
        in_specs=[
            pl.BlockSpec((1, 2, tb), lambda i: (i, 0, 0), memory_space=pltpu.SMEM),
            pl.BlockSpec((1, 2, tb), lambda i: (jnp.minimum(i + 1, n_tiles - 1), 0, 0), memory_space=pltpu.SMEM),
            pl.BlockSpec(memory_space=pl.ANY),
            pl.BlockSpec((tb, d), lambda i: (i, 0)),
            pl.BlockSpec((tb, META_LANES), lambda i: (i, 0)),
            pl.BlockSpec((1, d), lambda i: (0, 0)),
        ],
        out_specs=pl.BlockSpec((tb, d), lambda i: (i, 0)),
        out_shape=jax.ShapeDtypeStruct((n, d), F32),
        scratch_shapes=[pltpu.VMEM((2, tb, d), F32), pltpu.VMEM((2, tb, d), F32),
                        pltpu.SemaphoreType.DMA((2,))],
        compiler_params=pltpu.CompilerParams(
            dimension_semantics=("arbitrary",), vmem_limit_bytes=VMEM_LIMIT),
        name="combine",
    )(pos, pos, ys, x3, rw, g)


def _pick_tile(n, pref):
    t = min(pref, n)
    while n % t:
        t //= 2
    return t


def kernel(x, norm_g, pool_w, pool_scale, attn_w_in, attn_b_f, attn_w_out, ffn_w_gate, ffn_w_up, ffn_w_down,
           moe_w_router, moe_w_gate, moe_w_up, moe_w_down):
    b, s, d = x.shape
    n = b * s
    nh = d // HEAD_DIM
    tm = _pick_tile(s, 512)

    x = _layer0(x, norm_g[0], pool_w[0].astype(BF16), pool_scale[0],
                ffn_w_gate[0].astype(BF16), ffn_w_up[0].astype(BF16), ffn_w_down[0].astype(BF16), tm=tm)

    w_in = attn_w_in[0]
    w_f = jnp.zeros((d, LANES), F32).at[:, :nh].set(w_in[:, 3 * d:]).astype(BF16)
    b_f = jnp.zeros((1, LANES), F32).at[0, :nh].set(attn_b_f[0])
    w_qvt = jnp.concatenate([w_in[:, :d], w_in[:, 2 * d:3 * d]], axis=1).T.astype(BF16)
    qt, k, vt, c2, qn, kn = _qkv(x, norm_g[1, 0:1], w_in[:, d:2 * d].astype(BF16), w_qvt, w_f, b_f, tm=tm)
    first, stab, stab_ok = _first_live_block(c2, qn, kn, tq=tm, heads_per_step=ATTN_HEADS_PER_STEP)
    o = lax.cond(stab_ok,
                 functools.partial(_attention, tq=tm, fixed_stabiliser=True),
                 functools.partial(_attention, tq=tm, fixed_stabiliser=False),
                 qt, k, vt, first, stab)

    w_router = jnp.zeros((d, LANES), F32).at[:, :N_EXPERTS].set(moe_w_router[0])
    x3, ri, rw, counts = _attn_out(o.reshape(n, d), x.reshape(n, d), norm_g[1, 1:3],
                                   attn_w_out[0].astype(BF16), w_router, tm=tm)

    tme = _pick_tile(n, 512)
    counts = counts[0, :N_EXPERTS]
    padded = (counts + tme - 1) // tme * tme
    ends = jnp.cumsum(padded)
    starts = ends - padded
    n_rows = 2 * n + N_EXPERTS * tme
    pos = jnp.stack([starts[ri[:, 0]] + ri[:, 2], starts[ri[:, 1]] + ri[:, 3]], axis=0)
    tb = _pick_tile(n, 256)
    pos = pos.reshape(2, n // tb, tb).transpose(1, 0, 2)
    fill_hi = ends.at[N_EXPERTS - 1].set(n_rows)
    fill = jnp.stack([starts + counts, fill_hi], axis=0).astype(jnp.int32)
    tile_start = jnp.arange(n_rows // tme, dtype=jnp.int32) * tme
    tile_expert = jnp.minimum(jnp.sum(tile_start[:, None] >= ends[None, :], axis=1), N_EXPERTS - 1).astype(jnp.int32)
    n_tiles = (ends[-1:] // tme).astype(jnp.int32)

    xs = _dispatch(x3, norm_g[1, 2:3], pos, fill, n_rows, tb=tb)
    dffe = moe_w_gate.shape[-1]
    tf = dffe // 2 if (dffe // 2) % LANES == 0 else dffe
    ys = _experts(xs, tile_expert, n_tiles, moe_w_gate[0].astype(BF16), moe_w_up[0].astype(BF16),
                  moe_w_down[0].astype(BF16), tm=tme, tf=tf)
    out = _combine(ys, x3, rw, norm_g[1, 3:4], pos, tb=tb)
    return out.reshape(b, s, d)
```

```python
import functools

import jax
import jax.numpy as jnp
from jax import lax
from jax.experimental import pallas as pl
from jax.experimental.pallas import tpu as pltpu

F32 = jnp.float32
BF16 = jnp.bfloat16

RMS_EPS = 1e-6
HEAD_DIM = 64
POOL_WINDOWS = (2, 4, 8, 16)
POOL_HALO = 16
N_EXPERTS = 8
NEG_INF = -1e30
LANES = 128
VMEM_LIMIT = 56 * 1024 * 1024
ISSUE_UNROLL = 8
LOG2E = 1.4426950408889634
ATTN_HEADS_PER_STEP = 4
CHAIN_ROWS = 256
META_LANES = 8
STAB_LANE = 6


def _rms(x, g):
    ms = jnp.mean(x * x, axis=-1, keepdims=True)
    return x * lax.rsqrt(ms + RMS_EPS) * g


def _split3(c):
    hi = c.astype(BF16).astype(F32)
    r = c - hi
    mid = r.astype(BF16).astype(F32)
    lo = r - mid
    return hi, mid, lo


def _resident(shape):
    nd = len(shape)
    return pl.BlockSpec(shape, lambda *_: (0,) * nd, pipeline_mode=pl.Buffered(1))


def _layer0_kernel(x_ref, xp_ref, g_ref, pw_ref, ps_ref, wg_ref, wu_ref, wd_ref, o_ref,
                   buf_a, buf_b, *, tm, sub, ff_chunk):
    s = pl.program_id(1)
    d = x_ref.shape[-1]
    gd = d // len(POOL_WINDOWS)
    g = g_ref[...]
    dff = wg_ref.shape[1]
    lo = 8
    top = sub + POOL_HALO + lo
    body = lo + POOL_HALO
    zeros8 = jnp.zeros((lo, d), F32)
    halo = jnp.where(s > 0, _rms(xp_ref[0], g[0:1]), 0.0)

    for idx, r0 in enumerate(range(0, tm, sub)):
        x = x_ref[0, r0:r0 + sub, :]
        h = _rms(x, g[0:1])
        ba = buf_a.at[idx]
        bb = buf_b.at[idx]
        ba[0:lo, :] = zeros8
        bb[0:lo, :] = zeros8
        ba[lo:body, :] = halo
        ba[body:top, :] = h
        halo = h[sub - POOL_HALO:, :]

        def shifted_sum(src, k, c0):
            return src[lo:top, c0:] + src[lo - k:top - k, c0:]

        sums = [ba[body:top, 0:gd] + ba[body - 1:top - 1, 0:gd]]
        bb[lo:top, gd:] = shifted_sum(ba, 1, gd)
        sums.append(bb[body:top, gd:2 * gd] + bb[body - 2:top - 2, gd:2 * gd])
        ba[lo:top, 2 * gd:] = shifted_sum(bb, 2, 2 * gd)
        sums.append(ba[body:top, 2 * gd:3 * gd] + ba[body - 4:top - 4, 2 * gd:3 * gd])
        bb[lo:top, 3 * gd:] = shifted_sum(ba, 4, 3 * gd)
        sums.append(bb[body:top, 3 * gd:] + bb[body - 8:top - 8, 3 * gd:])

        pos = s * tm + r0 + lax.broadcasted_iota(jnp.int32, (sub, 1), 0)
        mixed = []
        for gi, w in enumerate(POOL_WINDOWS):
            count = jnp.minimum(pos + 1, w).astype(F32)
            pooled = sums[gi] / count - h[:, gi * gd:(gi + 1) * gd]
            mixed.append(jnp.dot(pooled.astype(BF16), pw_ref[gi], preferred_element_type=F32))
        y = jnp.concatenate(mixed, axis=-1) * ps_ref[...]
        x1 = x + _rms(y, g[1:2])

        h2 = _rms(x1, g[2:3]).astype(BF16)
        acc = jnp.zeros((sub, d), F32)
        for c0 in range(0, dff, ff_chunk):
            gate = jnp.dot(h2, wg_ref[:, c0:c0 + ff_chunk], preferred_element_type=F32)
            up = jnp.dot(h2, wu_ref[:, c0:c0 + ff_chunk], preferred_element_type=F32)
            act = (gate * jax.nn.sigmoid(gate) * up).astype(BF16)
            acc = acc + jnp.dot(act, wd_ref[c0:c0 + ff_chunk, :], preferred_element_type=F32)
        o_ref[0, r0:r0 + sub, :] = x1 + _rms(acc, g[3:4])


def _layer0(x, g4, pool_w, pool_scale, w_gate, w_up, w_down, *, tm):
    b, s, d = x.shape
    dff = w_gate.shape[1]
    ff_chunk = dff
    halo_per_tile = tm // POOL_HALO
    sub = min(CHAIN_ROWS, tm)
    kern = functools.partial(_layer0_kernel, tm=tm, sub=sub, ff_chunk=ff_chunk)
    return pl.pallas_call(
        kern,
        grid=(b, s // tm),
        in_specs=[
            pl.BlockSpec((1, tm, d), lambda bi, si: (bi, si, 0)),
            pl.BlockSpec((1, POOL_HALO, d), lambda bi, si: (bi, jnp.maximum(si * halo_per_tile - 1, 0), 0)),
            _resident((4, d)),
            _resident(pool_w.shape),
            _resident((1, d)),
            _resident(w_gate.shape),
            _resident(w_up.shape),
            _resident(w_down.shape),
        ],
        out_specs=pl.BlockSpec((1, tm, d), lambda bi, si: (bi, si, 0)),
        out_shape=jax.ShapeDtypeStruct((b, s, d), F32),
        scratch_shapes=[pltpu.VMEM((tm // sub, sub + POOL_HALO + 8, d), F32),
                        pltpu.VMEM((tm // sub, sub + POOL_HALO + 8, d), F32)],
        compiler_params=pltpu.CompilerParams(
            dimension_semantics=("arbitrary", "arbitrary"), vmem_limit_bytes=VMEM_LIMIT),
        name="layer0",
    )(x, x, g4, pool_w, pool_scale.reshape(1, d), w_gate, w_up, w_down)


def _qkv_kernel(x_ref, g_ref, wk_ref, wqvt_ref, wf_ref, bf_ref, tri_ref, hsel_ref,
                qt_ref, k_ref, vt_ref, c2_ref, qn_ref, kn_ref, carry_ref, *, tm, n_heads):
    s = pl.program_id(1)
    d = x_ref.shape[-1]
    nt = (((1,), (1,)), ((), ()))

    @pl.when(s == 0)
    def _():
        carry_ref[...] = jnp.zeros_like(carry_ref)

    sub = min(CHAIN_ROWS, tm)
    lane = lax.broadcasted_iota(jnp.int32, (sub, HEAD_DIM), 1)
    row = lax.broadcasted_iota(jnp.int32, (HEAD_DIM, sub), 0)
    aug_vt = jnp.where(row == 0, 1.0, 0.0)
    scale = HEAD_DIM ** -0.5 * LOG2E
    tri = tri_ref[...]
    carry = carry_ref[...]
    qn_max = None
    kn_max = None
    for r0 in range(0, tm, sub):
        rows = pl.ds(r0, sub)
        h = _rms(x_ref[0, rows, :], g_ref[...]).astype(BF16)
        kproj = jnp.dot(h, wk_ref[...], preferred_element_type=F32)
        qvt = lax.dot_general(wqvt_ref[...], h, nt, preferred_element_type=F32)
        z = jnp.dot(h, wf_ref[...], preferred_element_type=F32) + bf_ref[...]
        log_f = jnp.minimum(z, 0.0) - jnp.log(1.0 + jnp.exp(-jnp.abs(z)))

        c = carry
        for piece in _split3(log_f):
            c = c + jnp.dot(tri, piece.astype(BF16), preferred_element_type=F32)
        carry = c[sub - 1:sub, :]
        c = c * LOG2E
        c2_ref[0, rows, :] = c
        c_hi, c_mid, c_lo = _split3(c)
        ct_hi, ct_mid, ct_lo = _split3(c.T)

        q_sq = jnp.square(qvt[:d, :] * scale).reshape(n_heads, HEAD_DIM, sub)
        qn = jnp.max(jnp.sum(q_sq, axis=1), axis=1, keepdims=True)
        k_sq = jnp.dot(jnp.square(kproj).astype(BF16), hsel_ref[...], preferred_element_type=F32)
        kn = jnp.max(k_sq, axis=0, keepdims=True)
        qn_max = qn if qn_max is None else jnp.maximum(qn_max, qn)
        kn_max = kn if kn_max is None else jnp.maximum(kn_max, kn)
        for hd in range(n_heads):
            aug_k = jnp.where(lane == 3, -c_hi[:, hd:hd + 1], jnp.where(
                lane == 4, -c_mid[:, hd:hd + 1], jnp.where(
                    lane == 5, -c_lo[:, hd:hd + 1], jnp.where((lane < 3) | (lane == STAB_LANE), 1.0, 0.0))))
            aug_qt = jnp.where(row == 0, ct_hi[hd:hd + 1, :], jnp.where(
                row == 1, ct_mid[hd:hd + 1, :], jnp.where(
                    row == 2, ct_lo[hd:hd + 1, :], jnp.where(row < 6, 1.0, 0.0))))
            c0 = hd * HEAD_DIM
            k_ref[0, hd, rows, :] = jnp.concatenate([kproj[:, c0:c0 + HEAD_DIM], aug_k], axis=-1).astype(BF16)
            qt_ref[0, hd, :, rows] = jnp.concatenate(
                [qvt[c0:c0 + HEAD_DIM, :] * scale, aug_qt], axis=0).astype(BF16)
            vt_ref[0, hd, :, rows] = jnp.concatenate(
                [qvt[d + c0:d + c0 + HEAD_DIM, :], aug_vt], axis=0).astype(BF16)

    carry_ref[...] = carry
    qn_ref[0, 0] = jnp.broadcast_to(qn_max, qn_ref.shape[2:])
    kn_ref[0, 0] = jnp.broadcast_to(kn_max, kn_ref.shape[2:])


def _qkv(x, g, w_k, w_qvt, w_f, b_f, *, tm):
    b, s, d = x.shape
    n_heads = d // HEAD_DIM
    dk = 2 * HEAD_DIM
    sub = min(CHAIN_ROWS, tm)
    tri = jnp.tril(jnp.ones((sub, sub), BF16))
    hsel =(jnp.arange(d)[:, None] // HEAD_DIM == jnp.arange(LANES)[None, :]).astype(BF16)
    kern = functools.partial(_qkv_kernel, tm=tm, n_heads=n_heads)
    row_sds = jax.ShapeDtypeStruct((b, n_heads, s, dk), BF16)
    col_sds = jax.ShapeDtypeStruct((b, n_heads, dk, s), BF16)
    row_spec = pl.BlockSpec((1, n_heads, tm, dk), lambda bi, si: (bi, 0, si, 0))
    col_spec = pl.BlockSpec((1, n_heads, dk, tm), lambda bi, si: (bi, 0, 0, si))
    return pl.pallas_call(
        kern,
        grid=(b, s // tm),
        in_specs=[
            pl.BlockSpec((1, tm, d), lambda bi, si: (bi, si, 0)),
            _resident((1, d)),
            _resident(w_k.shape),
            _resident(w_qvt.shape),
            _resident(w_f.shape),
            _resident(b_f.shape),
            _resident((sub, sub)),
            _resident((d, LANES)),
        ],
        out_specs=[col_spec, row_spec, col_spec,
                   pl.BlockSpec((1, tm, LANES), lambda bi, si: (bi, si, 0)),
                   pl.BlockSpec((1, 1, n_heads, LANES), lambda bi, si: (bi, si, 0, 0)),
                   pl.BlockSpec((1, 1, 8, LANES), lambda bi, si: (bi, si, 0, 0))],
        out_shape=[col_sds, row_sds, col_sds,
                   jax.ShapeDtypeStruct((b, s, LANES), F32),
                   jax.ShapeDtypeStruct((b, s // tm, n_heads, LANES), F32),
                   jax.ShapeDtypeStruct((b, s // tm, 8, LANES), F32)],
        scratch_shapes=[pltpu.VMEM((1, LANES), F32)],
        compiler_params=pltpu.CompilerParams(
            dimension_semantics=("arbitrary", "arbitrary"), vmem_limit_bytes=VMEM_LIMIT),
        name="qkv",
    )(x, g, w_k, w_qvt, w_f, b_f, tri, hsel)


def _attn_kernel(first_ref, stab_ref, perm_ref, *refs, tq, tk, heads_per_step, fixed_stabiliser):
    del perm_ref
    qt_refs = refs[:heads_per_step]
    k_refs = refs[heads_per_step:2 * heads_per_step]
    vt_refs = refs[2 * heads_per_step:3 * heads_per_step]
    eye_ref, o_ref, m_ref, acc_ref, s0_ref, s1_ref = refs[3 * heads_per_step:]
    qi = pl.program_id(2)
    step = (pl.program_id(0) * pl.num_programs(1) + pl.program_id(1)) * pl.num_programs(2) + qi
    first = first_ref[step]
    m_ref[...] = jnp.full(m_ref.shape, NEG_INF, F32)
    acc_ref[...] = jnp.zeros(acc_ref.shape, F32)

    if fixed_stabiliser:
        row = lax.broadcasted_iota(jnp.int32, qt_refs[0].shape[2:], 0)
        head0 = (pl.program_id(0) * pl.num_programs(1) + pl.program_id(1)) * heads_per_step
        qts = [jnp.where(row == HEAD_DIM + STAB_LANE, (-stab_ref[head0 + hh]).astype(BF16), qt_refs[hh][0, 0])
               for hh in range(heads_per_step)]
    else:
        qts = [qt_refs[hh][0, 0] for hh in range(heads_per_step)]

    def scores(sub, dst_ref):
        start = pl.multiple_of(sub * tk, tk)
        for hh in range(heads_per_step):
            k = k_refs[hh][0, 0, pl.ds(start, tk), :]
            dst_ref[hh] = jnp.dot(k, qts[hh], preferred_element_type=F32)

    def consume(sub, src_ref, first_key=None):
        start = pl.multiple_of(sub * tk, tk)
        for hh in range(heads_per_step):
            st = src_ref[hh]
            if first_key is not None:
                key = lax.broadcasted_iota(jnp.int32, (tk, tq), 0) + first_key
                qry = lax.broadcasted_iota(jnp.int32, (tk, tq), 1)
                st = jnp.where(key <= qry, st, NEG_INF)
            vt = vt_refs[hh][0, 0, :, pl.ds(start, tk)]
            if fixed_stabiliser:
                acc_ref[hh] += jnp.dot(vt, jnp.exp2(st).astype(BF16), preferred_element_type=F32)
                continue
            m_old = m_ref[hh]
            m_new = jnp.maximum(m_old, jnp.max(st, axis=0, keepdims=True))
            pt = jnp.exp2(st - m_new).astype(BF16)
            alpha = jnp.exp2(m_old - m_new)
            acc_ref[hh] = alpha * acc_ref[hh] + jnp.dot(vt, pt, preferred_element_type=F32)
            m_ref[hh] = m_new

    subs = tq // tk
    assert subs == 2
    scores(2 * first, s0_ref)

    def trip(j, carry):
        scores(2 * j + 1, s1_ref)
        consume(2 * j, s0_ref)
        scores(2 * j + 2, s0_ref)
        consume(2 * j + 1, s1_ref)
        return carry

    n_blocks = qi - first

    def double_trip(p, carry):
        trip(first + 2 * p, carry)
        return trip(first + 2 * p + 1, carry)

    lax.fori_loop(0, n_blocks // 2, double_trip, 0)

    @pl.when(n_blocks % 2 == 1)
    def _():
        trip(qi - 1, 0)

    scores(2 * qi + 1, s1_ref)
    consume(2 * qi, s0_ref, first_key=0)
    consume(2 * qi + 1, s1_ref, first_key=tk)
    outs = []
    for hh in range(heads_per_step):
        acc = acc_ref[hh]
        outs.append((acc[:HEAD_DIM, :] / acc[HEAD_DIM:HEAD_DIM + 1, :]).astype(o_ref.dtype))
    o_t = jnp.concatenate(outs, axis=0)
    o_ref[0] = lax.dot_general(eye_ref[...], o_t, (((1,), (1,)), ((), ())),
                               preferred_element_type=F32).astype(o_ref.dtype)


UNDERFLOW_LOG2 = 126.0
SKIP_MARGIN_LOG2 = 8.0
STAB_MAX_SPREAD_LOG2 = 100.0


def _first_live_block(c2, qn, kn, *, tq, heads_per_step):
    b, s, _ = c2.shape
    nh = qn.shape[2]
    nq = s // tq
    qk = jnp.sqrt(jnp.max(qn[:, :, :, 0], axis=1) * jnp.max(kn[:, :, 0, :nh], axis=1)) * 1.03
    gap = 2.0 * qk + UNDERFLOW_LOG2 + SKIP_MARGIN_LOG2
    c_query = c2[:, 0::tq, :nh]
    c_key = c2[:, tq - 1::tq, :nh]
    dead = (c_key[:, None, :, :] - c_query[:, :, None, :]) > gap[:, None, None, :]
    dead = dead & (jnp.arange(nq)[None, None, :, None] < jnp.arange(nq)[None, :, None, None])
    first = jnp.sum(dead, axis=2).astype(jnp.int32)
    perm = jnp.argsort(jnp.sum(first, axis=1), axis=-1).astype(jnp.int32)
    first = jnp.take_along_axis(first, perm[:, None, :], axis=2)
    first = jnp.min(first.reshape(b, nq, nh // heads_per_step, heads_per_step), axis=-1)
    stab = jnp.take_along_axis(qk.astype(BF16).astype(F32), perm, axis=1).reshape(-1)
    stab_ok = jnp.max(2.0 * qk) + SKIP_MARGIN_LOG2 < STAB_MAX_SPREAD_LOG2
    return first.transpose(0, 2, 1).reshape(-1), stab, stab_ok, perm


def _attention(qt, k, vt, first, stab, perm, *, tq, fixed_stabiliser):
    b, nh, s, dk = k.shape
    hps = ATTN_HEADS_PER_STEP
    tk = tq // 2
    kern = functools.partial(_attn_kernel, tq=tq, tk=tk, heads_per_step=hps, fixed_stabiliser=fixed_stabiliser)

    def head_map(r, per_q_block, bi, hi, qi, f, m, p):
        head = p[bi * nh + hi * hps + r]
        return (bi, head, 0, qi) if per_q_block else (bi, head, 0, 0)

    return pl.pallas_call(
        kern,
        grid_spec=pltpu.PrefetchScalarGridSpec(
            num_scalar_prefetch=3,
            grid=(b, nh // hps, s // tq),
            in_specs=(
                [pl.BlockSpec((1, 1, dk, tq), functools.partial(head_map, r, True)) for r in range(hps)]
                + [pl.BlockSpec((1, 1, s, dk), functools.partial(head_map, r, False)) for r in range(hps)]
                + [pl.BlockSpec((1, 1, dk, s), functools.partial(head_map, r, False)) for r in range(hps)]
                + [pl.BlockSpec((tq, tq), lambda bi, hi, qi, f, m, p: (0, 0), pipeline_mode=pl.Buffered(1))]),
            out_specs=pl.BlockSpec((1, tq, hps * HEAD_DIM), lambda bi, hi, qi, f, m, p: (bi, qi, hi)),
            scratch_shapes=[pltpu.VMEM((hps, 1, tq), F32), pltpu.VMEM((hps, dk, tq), F32),
                            pltpu.VMEM((hps, tk, tq), F32), pltpu.VMEM((hps, tk, tq), F32)],
        ),
        out_shape=jax.ShapeDtypeStruct((b, s, nh * HEAD_DIM), BF16),
        compiler_params=pltpu.CompilerParams(
            dimension_semantics=("arbitrary", "arbitrary", "arbitrary"), vmem_limit_bytes=VMEM_LIMIT),
        name="attention",
    )(first, stab, perm.reshape(-1), *([qt] * hps), *([k] * hps), *([vt] * hps), jnp.eye(tq, dtype=BF16))


def _attn_out_kernel(o_ref, x_ref, g_ref, wo_ref, wrh_ref, wrl_ref, ltri_ref, x3_ref, ri_ref, rw_ref, cnt_ref,
                     carry_ref, *, tm, sub):
    i = pl.program_id(0)
    g = g_ref[...]

    @pl.when(i == 0)
    def _():
        carry_ref[...] = jnp.zeros_like(carry_ref)

    carry = carry_ref[...]
    lane = lax.broadcasted_iota(jnp.int32, (sub, LANES), 1)
    for r0 in range(0, tm, sub):
        rows = pl.ds(r0, sub)
        y = jnp.dot(o_ref[rows, :], wo_ref[0], preferred_element_type=F32)
        x3 = x_ref[rows, :] + _rms(y, g[0:1])
        x3_ref[rows, :] = x3
        h = _rms(x3, g[1:2])
        h_hi = h.astype(BF16)
        h_lo = (h - h_hi.astype(F32)).astype(BF16)
        logits = (jnp.dot(h_hi, wrh_ref[...], preferred_element_type=F32)
                  + jnp.dot(h_lo, wrh_ref[...], preferred_element_type=F32)
                  + jnp.dot(h_hi, wrl_ref[...], preferred_element_type=F32))
        logits = jnp.where(lane < N_EXPERTS, logits, -jnp.inf)
        m1 = jnp.max(logits, axis=-1, keepdims=True)
        e1 = jnp.min(jnp.where(logits == m1, lane, LANES), axis=-1, keepdims=True)
        rest = jnp.where(lane == e1, -jnp.inf, logits)
        m2 = jnp.max(rest, axis=-1, keepdims=True)
        e2 = jnp.min(jnp.where(rest == m2, lane, LANES), axis=-1, keepdims=True)
        t = jnp.exp(m2 - m1)
        w1 = 1.0 / (1.0 + t)
        w2 = t / (1.0 + t)

        hot1 = lane == e1
        hot2 = lane == e2
        cnt = jnp.where(hot1 | hot2, 1.0, 0.0)
        before = jnp.dot(ltri_ref[...], cnt.astype(BF16), preferred_element_type=F32) + carry
        r1 = jnp.sum(jnp.where(hot1, before, 0.0), axis=-1, keepdims=True)
        r2 = jnp.sum(jnp.where(hot2, before, 0.0), axis=-1, keepdims=True)
        carry = carry + jnp.sum(cnt, axis=0, keepdims=True)
        ri = jnp.where(lane == 0, e1, jnp.where(lane == 1, e2, jnp.where(
            lane == 2, r1.astype(jnp.int32), jnp.where(lane == 3, r2.astype(jnp.int32), 0))))
        ri_ref[rows, :] = ri[:, :META_LANES]
        rw_ref[rows, :] = jnp.where(lane == 0, w1, jnp.where(lane == 1, w2, 0.0))[:, :META_LANES]

    carry_ref[...] = carry
    cnt_ref[...] = jnp.broadcast_to(carry, cnt_ref.shape).astype(jnp.int32)


def _attn_out(o, x, g2, w_out, w_router_pad, *, tm, tiles_per_batch):
    n, d = x.shape
    sub = min(CHAIN_ROWS, tm)
    ltri = jnp.tril(jnp.ones((sub, sub), BF16), k=-1)
    wr_hi = w_router_pad.astype(BF16)
    wr_lo = (w_router_pad - wr_hi.astype(F32)).astype(BF16)
    kern = functools.partial(_attn_out_kernel, tm=tm, sub=sub)
    row_spec = pl.BlockSpec((tm, d), lambda i: (i, 0))
    meta_spec = pl.BlockSpec((tm, META_LANES), lambda i: (i, 0))
    return pl.pallas_call(
        kern,
        grid=(n // tm,),
        in_specs=[row_spec, row_spec, _resident((2, d)),
                  pl.BlockSpec((1, d, d), lambda i: (i // tiles_per_batch, 0, 0)),
                  _resident(wr_hi.shape), _resident(wr_lo.shape), _resident((sub, sub))],
        out_specs=[row_spec, meta_spec, meta_spec, pl.BlockSpec((8, LANES), lambda i: (0, 0))],
        out_shape=[jax.ShapeDtypeStruct((n, d), F32),
                   jax.ShapeDtypeStruct((n, META_LANES), jnp.int32),
                   jax.ShapeDtypeStruct((n, META_LANES), F32),
                   jax.ShapeDtypeStruct((8, LANES), jnp.int32)],
        scratch_shapes=[pltpu.VMEM((1, LANES), F32)],
        compiler_params=pltpu.CompilerParams(
            dimension_semantics=("arbitrary",), vmem_limit_bytes=VMEM_LIMIT),
        name="attn_out",
    )(o, x, g2, w_out, wr_hi, wr_lo, ltri)


def _row_copy(src, dst, sem):
    return pltpu.make_async_copy(src, dst, sem)


def _dispatch_kernel(fill_ref, pos_ref, x_ref, g_ref, xs_ref, h_buf, z_buf, sem, zsem, *, tb):
    i = pl.program_id(0)
    last = pl.num_programs(0) - 1
    slot = lax.rem(i, 2)
    h_buf[slot] = _rms(x_ref[...], g_ref[...])

    def issue(t, _):
        for which in range(2):
            p = pos_ref[0, which, t]
            _row_copy(h_buf.at[slot, pl.ds(t, 1), :], xs_ref.at[pl.ds(p, 1), :], sem.at[slot]).start()
        return 0

    lax.fori_loop(0, tb, issue, 0, unroll=ISSUE_UNROLL)

    def drain(s):
        for _ in range(2 * tb):
            _row_copy(h_buf.at[s, pl.ds(0, 1), :], xs_ref.at[pl.ds(0, 1), :], sem.at[s]).wait()

    @pl.when(i == 0)
    def _():
        z_buf[...] = jnp.zeros_like(z_buf)
        for e in range(N_EXPERTS):
            lo = fill_ref[0, e]
            hi = fill_ref[1, e]

            def fill(r, _):
                _row_copy(z_buf, xs_ref.at[pl.ds(r, 1), :], zsem).start()
                return 0

            lax.fori_loop(lo, hi, fill, 0)

            def fill_done(r, _):
                _row_copy(z_buf, xs_ref.at[pl.ds(0, 1), :], zsem).wait()
                return 0

            lax.fori_loop(lo, hi, fill_done, 0)

    @pl.when(i > 0)
    def _():
        drain(1 - slot)

    @pl.when(i == last)
    def _():
        drain(slot)


def _dispatch(x3, g, pos, fill, n_rows, *, tb):
    n, d = x3.shape
    kern = functools.partial(_dispatch_kernel, tb=tb)
    return pl.pallas_call(
        kern,
        grid_spec=pltpu.PrefetchScalarGridSpec(
            num_scalar_prefetch=1,
            grid=(n // tb,),
            in_specs=[
                pl.BlockSpec((1, 2, tb), lambda i, f: (i, 0, 0), memory_space=pltpu.SMEM),
                pl.BlockSpec((tb, d), lambda i, f: (i, 0)),
                pl.BlockSpec((1, d), lambda i, f: (0, 0)),
            ],
            out_specs=pl.BlockSpec(memory_space=pl.ANY),
            scratch_shapes=[pltpu.VMEM((2, tb, d), F32), pltpu.VMEM((1, d), F32),
                            pltpu.SemaphoreType.DMA((2,)), pltpu.SemaphoreType.DMA(())],
        ),
        out_shape=jax.ShapeDtypeStruct((n_rows, d), F32),
        compiler_params=pltpu.CompilerParams(
            dimension_semantics=("arbitrary",), vmem_limit_bytes=VMEM_LIMIT),
        name="dispatch",
    )(fill, pos, x3, g)


def _experts_kernel(te_ref, nt_ref, xs_ref, wg_ref, wu_ref, wd_ref, ys_ref, xb_ref, acc_ref, *, nj):
    i = pl.program_id(0)
    j = pl.program_id(1)
    live = i < nt_ref[0]
    tm = xb_ref.shape[0]
    sub = min(CHAIN_ROWS, tm)

    def partial_out(xb):
        gate = jnp.dot(xb, wg_ref[0], preferred_element_type=F32)
        up = jnp.dot(xb, wu_ref[0], preferred_element_type=F32)
        act = (gate * jax.nn.sigmoid(gate) * up).astype(BF16)
        return jnp.dot(act, wd_ref[0], preferred_element_type=F32)

    @pl.when(live & (j == 0))
    def _():
        for r0 in range(0, tm, sub):
            xb = xs_ref[r0:r0 + sub, :].astype(BF16)
            if nj > 1:
                xb_ref[r0:r0 + sub, :] = xb
                acc_ref[r0:r0 + sub, :] = partial_out(xb)
            else:
                ys_ref[r0:r0 + sub, :] = partial_out(xb)

    if nj > 2:
        @pl.when(live & (j > 0) & (j < nj - 1))
        def _():
            for r0 in range(0, tm, sub):
                acc_ref[r0:r0 + sub, :] += partial_out(xb_ref[r0:r0 + sub, :])

    if nj > 1:
        @pl.when(live & (j == nj - 1))
        def _():
            for r0 in range(0, tm, sub):
                ys_ref[r0:r0 + sub, :] = acc_ref[r0:r0 + sub, :] + partial_out(xb_ref[r0:r0 + sub, :])

    @pl.when(jnp.logical_not(live) & (j == nj - 1))
    def _():
        ys_ref[...] = jnp.zeros_like(ys_ref)


def _experts(xs, tile_expert, n_tiles, w_gate, w_up, w_down, *, tm, tf):
    n_rows, d = xs.shape
    dff = w_gate.shape[-1]
    max_tiles = n_rows // tm

    def row_map(i, j, te, nt):
        return (jnp.minimum(i, nt[0] - 1), 0)

    def wcol_map(i, j, te, nt):
        return (te[i], 0, j)

    def wrow_map(i, j, te, nt):
        return (te[i], j, 0)

    return pl.pallas_call(
        functools.partial(_experts_kernel, nj=dff // tf),
        grid_spec=pltpu.PrefetchScalarGridSpec(
            num_scalar_prefetch=2,
            grid=(max_tiles, dff // tf),
            in_specs=[
                pl.BlockSpec((tm, d), row_map),
                pl.BlockSpec((1, d, tf), wcol_map),
                pl.BlockSpec((1, d, tf), wcol_map),
                pl.BlockSpec((1, tf, d), wrow_map),
            ],
            out_specs=pl.BlockSpec((tm, d), lambda i, j, te, nt: (i, 0)),
            scratch_shapes=[pltpu.VMEM((tm, d), BF16), pltpu.VMEM((tm, d), F32)],
        ),
        out_shape=jax.ShapeDtypeStruct((n_rows, d), F32),
        compiler_params=pltpu.CompilerParams(
            dimension_semantics=("arbitrary", "arbitrary"), vmem_limit_bytes=VMEM_LIMIT),
        name="experts",
    )(tile_expert, n_tiles, xs, w_gate, w_up, w_down)


def _combine_kernel(pos_ref, posn_ref, ys_ref, x_ref, rw_ref, g_ref, o_ref, a_buf, b_buf, sem, *, tb):
    i = pl.program_id(0)
    last = pl.num_programs(0) - 1
    slot = lax.rem(i, 2)

    def gather(idx_ref, s):
        def issue(t, _):
            _row_copy(ys_ref.at[pl.ds(idx_ref[0, 0, t], 1), :], a_buf.at[s, pl.ds(t, 1), :], sem.at[s]).start()
            _row_copy(ys_ref.at[pl.ds(idx_ref[0, 1, t], 1), :], b_buf.at[s, pl.ds(t, 1), :], sem.at[s]).start()
            return 0

        lax.fori_loop(0, tb, issue, 0, unroll=ISSUE_UNROLL)

    @pl.when(i == 0)
    def _():
        gather(pos_ref, slot)

    @pl.when(i < last)
    def _():
        gather(posn_ref, 1 - slot)

    for _ in range(2 * tb):
        _row_copy(ys_ref.at[pl.ds(0, 1), :], a_buf.at[slot, pl.ds(0, 1), :], sem.at[slot]).wait()
    rw = rw_ref[...]
    y = rw[:, 0:1] * a_buf[slot] + rw[:, 1:2] * b_buf[slot]
    o_ref[...] = x_ref[...] + _rms(y, g_ref[...])


def _combine(ys, x3, rw, g, pos, *, tb):
    n, d = x3.shape
    n_tiles = n // tb
    kern = functools.partial(_combine_kernel, tb=tb)
    return pl.pallas_call(
        kern,
        grid=(n_tiles,),
        in_specs=[
            pl.BlockSpec((1, 2, tb), lambda i: (i, 0, 0), memory_space=pltpu.SMEM),
            pl.BlockSpec((1, 2, tb), lambda i: (jnp.minimum(i + 1, n_tiles - 1), 0, 0), memory_space=pltpu.SMEM),
            pl.BlockSpec(memory_space=pl.ANY),
            pl.BlockSpec((tb, d), lambda i: (i, 0)),
            pl.BlockSpec((tb, META_LANES), lambda i: (i, 0)),
            pl.BlockSpec((1, d), lambda i: (0, 0)),
        ],
        out_specs=pl.BlockSpec((tb, d), lambda i: (i, 0)),
        out_shape=jax.ShapeDtypeStruct((n, d), F32),
        scratch_shapes=[pltpu.VMEM((2, tb, d), F32), pltpu.VMEM((2, tb, d), F32),
                        pltpu.SemaphoreType.DMA((2,))],
        compiler_params=pltpu.CompilerParams(
            dimension_semantics=("arbitrary",), vmem_limit_bytes=VMEM_LIMIT),
        name="combine",
    )(pos, pos, ys, x3, rw, g)


def _pick_tile(n, pref):
    t = min(pref, n)
    while n % t:
        t //= 2
    return t


def kernel(x, norm_g, pool_w, pool_scale, attn_w_in, attn_b_f, attn_w_out, ffn_w_gate, ffn_w_up, ffn_w_down,
           moe_w_router, moe_w_gate, moe_w_up, moe_w_down):
    b, s, d = x.shape
    n = b * s
    nh = d // HEAD_DIM
    tq = _pick_tile(s, 512)
    tm = _pick_tile(s, 512)
    tm_qkv = _pick_tile(s, 1024)

    x = _layer0(x, norm_g[0], pool_w[0].astype(BF16), pool_scale[0],
                ffn_w_gate[0].astype(BF16), ffn_w_up[0].astype(BF16), ffn_w_down[0].astype(BF16), tm=tm)

    w_in = attn_w_in[0]
    w_f = jnp.zeros((d, LANES), F32).at[:, :nh].set(w_in[:, 3 * d:]).astype(BF16)
    b_f = jnp.zeros((1, LANES), F32).at[0, :nh].set(attn_b_f[0])
    w_qvt = jnp.concatenate([w_in[:, :d], w_in[:, 2 * d:3 * d]], axis=1).T.astype(BF16)
    qt, k, vt, c2, qn, kn = _qkv(x, norm_g[1, 0:1], w_in[:, d:2 * d].astype(BF16), w_qvt, w_f, b_f, tm=tm_qkv)
    first, stab, stab_ok, perm = _first_live_block(c2, qn, kn, tq=tq, heads_per_step=ATTN_HEADS_PER_STEP)
    o = lax.cond(stab_ok,
                 functools.partial(_attention, tq=tq, fixed_stabiliser=True),
                 functools.partial(_attention, tq=tq, fixed_stabiliser=False),
                 qt, k, vt, first, stab, perm)
    w_out = attn_w_out[0].astype(BF16).reshape(nh, HEAD_DIM, d)[perm].reshape(b, d, d)

    w_router = jnp.zeros((d, LANES), F32).at[:, :N_EXPERTS].set(moe_w_router[0])
    x3, ri, rw, counts = _attn_out(o.reshape(n, d), x.reshape(n, d), norm_g[1, 1:3], w_out, w_router,
                                   tm=tm, tiles_per_batch=s // tm)

    tme = _pick_tile(n, 512)
    counts = counts[0, :N_EXPERTS]
    padded = (counts + tme - 1) // tme * tme
    ends = jnp.cumsum(padded)
    starts = ends - padded
    n_rows = 2 * n + N_EXPERTS * tme
    pos = jnp.stack([starts[ri[:, 0]] + ri[:, 2], starts[ri[:, 1]] + ri[:, 3]], axis=0)
    tb = _pick_tile(n, 256)
    pos = pos.reshape(2, n // tb, tb).transpose(1, 0, 2)
    fill_hi = ends.at[N_EXPERTS - 1].set(n_rows)
    fill = jnp.stack([starts + counts, fill_hi], axis=0).astype(jnp.int32)
    tile_start = jnp.arange(n_rows // tme, dtype=jnp.int32) * tme
    tile_expert = jnp.minimum(jnp.sum(tile_start[:, None] >= ends[None, :], axis=1), N_EXPERTS - 1).astype(jnp.int32)
    n_tiles = (ends[-1:] // tme).astype(jnp.int32)

    xs = _dispatch(x3, norm_g[1, 2:3], pos, fill, n_rows, tb=tb)
    dffe = moe_w_gate.shape[-1]
    tf = dffe // 2 if (dffe // 2) % LANES == 0 else dffe
    ys = _experts(xs, tile_expert, n_tiles, moe_w_gate[0].astype(BF16), moe_w_up[0].astype(BF16),
                  moe_w_down[0].astype(BF16), tm=tme, tf=tf)
    out = _combine(ys, x3, rw, norm_g[1, 3:4], pos, tb=tb)
    return out.reshape(b, s, d)
```

```python
import functools

import jax
import jax.numpy as jnp
from jax import lax
from jax.experimental import pallas as pl
from jax.experimental.pallas import tpu as pltpu

F32 = jnp.float32
BF16 = jnp.bfloat16

RMS_EPS = 1e-6
HEAD_DIM = 64
POOL_WINDOWS = (2, 4, 8, 16)
POOL_HALO = 16
N_EXPERTS = 8
NEG_INF = -1e30
LANES = 128
VMEM_LIMIT = 56 * 1024 * 1024
ISSUE_UNROLL = 8
LOG2E = 1.4426950408889634
ATTN_HEADS_PER_STEP = 4
CHAIN_ROWS = 256
META_LANES = 8
STAB_LANE = 6


def _rms(x, g):
    ms = jnp.mean(x * x, axis=-1, keepdims=True)
    return x * lax.rsqrt(ms + RMS_EPS) * g


def _split3(c):
    hi = c.astype(BF16).astype(F32)
    r = c - hi
    mid = r.astype(BF16).astype(F32)
    lo = r - mid
    return hi, mid, lo


def _resident(shape):
    nd = len(shape)
    return pl.BlockSpec(shape, lambda *_: (0,) * nd, pipeline_mode=pl.Buffered(1))


def _layer0_kernel(x_ref, xp_ref, g_ref, pw_ref, ps_ref, wg_ref, wu_ref, wd_ref, o_ref,
                   buf_a, buf_b, *, tm, sub, ff_chunk):
    s = pl.program_id(1)
    d = x_ref.shape[-1]
    gd = d // len(POOL_WINDOWS)
    g = g_ref[...]
    dff = wg_ref.shape[1]
    lo = 8
    top = sub + POOL_HALO + lo
    body = lo + POOL_HALO
    zeros8 = jnp.zeros((lo, d), F32)
    halo = jnp.where(s > 0, _rms(xp_ref[0], g[0:1]), 0.0)

    for idx, r0 in enumerate(range(0, tm, sub)):
        x = x_ref[0, r0:r0 + sub, :]
        h = _rms(x, g[0:1])
        ba = buf_a.at[idx]
        bb = buf_b.at[idx]
        ba[0:lo, :] = zeros8
        bb[0:lo, :] = zeros8
        ba[lo:body, :] = halo
        ba[body:top, :] = h
        halo = h[sub - POOL_HALO:, :]

        def shifted_sum(src, k, c0):
            return src[lo:top, c0:] + src[lo - k:top - k, c0:]

        sums = [ba[body:top, 0:gd] + ba[body - 1:top - 1, 0:gd]]
        bb[lo:top, gd:] = shifted_sum(ba, 1, gd)
        sums.append(bb[body:top, gd:2 * gd] + bb[body - 2:top - 2, gd:2 * gd])
        ba[lo:top, 2 * gd:] = shifted_sum(bb, 2, 2 * gd)
        sums.append(ba[body:top, 2 * gd:3 * gd] + ba[body - 4:top - 4, 2 * gd:3 * gd])
        bb[lo:top, 3 * gd:] = shifted_sum(ba, 4, 3 * gd)
        sums.append(bb[body:top, 3 * gd:] + bb[body - 8:top - 8, 3 * gd:])

        pos = s * tm + r0 + lax.broadcasted_iota(jnp.int32, (sub, 1), 0)
        mixed = []
        for gi, w in enumerate(POOL_WINDOWS):
            count = jnp.minimum(pos + 1, w).astype(F32)
            pooled = sums[gi] / count - h[:, gi * gd:(gi + 1) * gd]
            mixed.append(jnp.dot(pooled.astype(BF16), pw_ref[gi], preferred_element_type=F32))
        y = jnp.concatenate(mixed, axis=-1) * ps_ref[...]
        x1 = x + _rms(y, g[1:2])

        h2 = _rms(x1, g[2:3]).astype(BF16)
        acc = jnp.zeros((sub, d), F32)
        for c0 in range(0, dff, ff_chunk):
            gate = jnp.dot(h2, wg_ref[:, c0:c0 + ff_chunk], preferred_element_type=F32)
            up = jnp.dot(h2, wu_ref[:, c0:c0 + ff_chunk], preferred_element_type=F32)
            act = (gate * jax.nn.sigmoid(gate) * up).astype(BF16)
            acc = acc + jnp.dot(act, wd_ref[c0:c0 + ff_chunk, :], preferred_element_type=F32)
        o_ref[0, r0:r0 + sub, :] = x1 + _rms(acc, g[3:4])


def _layer0(x, g4, pool_w, pool_scale, w_gate, w_up, w_down, *, tm):
    b, s, d = x.shape
    dff = w_gate.shape[1]
    ff_chunk = dff
    halo_per_tile = tm // POOL_HALO
    sub = min(CHAIN_ROWS, tm)
    kern = functools.partial(_layer0_kernel, tm=tm, sub=sub, ff_chunk=ff_chunk)
    return pl.pallas_call(
        kern,
        grid=(b, s // tm),
        in_specs=[
            pl.BlockSpec((1, tm, d), lambda bi, si: (bi, si, 0)),
            pl.BlockSpec((1, POOL_HALO, d), lambda bi, si: (bi, jnp.maximum(si * halo_per_tile - 1, 0), 0)),
            _resident((4, d)),
            _resident(pool_w.shape),
            _resident((1, d)),
            _resident(w_gate.shape),
            _resident(w_up.shape),
            _resident(w_down.shape),
        ],
        out_specs=pl.BlockSpec((1, tm, d), lambda bi, si: (bi, si, 0)),
        out_shape=jax.ShapeDtypeStruct((b, s, d), F32),
        scratch_shapes=[pltpu.VMEM((tm // sub, sub + POOL_HALO + 8, d), F32),
                        pltpu.VMEM((tm // sub, sub + POOL_HALO + 8, d), F32)],
        compiler_params=pltpu.CompilerParams(
            dimension_semantics=("arbitrary", "arbitrary"), vmem_limit_bytes=VMEM_LIMIT),
        name="layer0",
    )(x, x, g4, pool_w, pool_scale.reshape(1, d), w_gate, w_up, w_down)


def _qkv_kernel(x_ref, g_ref, wk_ref, wqvt_ref, wf_ref, bf_ref, tri_ref, hsel_ref,
                qt_ref, k_ref, vt_ref, c2_ref, qn_ref, kn_ref, carry_ref, *, tm, n_heads):
    s = pl.program_id(1)
    d = x_ref.shape[-1]
    nt = (((1,), (1,)), ((), ()))

    @pl.when(s == 0)
    def _():
        carry_ref[...] = jnp.zeros_like(carry_ref)

    sub = min(CHAIN_ROWS, tm)
    lane = lax.broadcasted_iota(jnp.int32, (sub, HEAD_DIM), 1)
    row = lax.broadcasted_iota(jnp.int32, (HEAD_DIM, sub), 0)
    aug_vt = jnp.where(row == 0, 1.0, 0.0)
    scale = HEAD_DIM ** -0.5 * LOG2E
    tri = tri_ref[...]
    carry = carry_ref[...]
    qn_max = None
    kn_max = None
    for r0 in range(0, tm, sub):
        rows = pl.ds(r0, sub)
        h = _rms(x_ref[0, rows, :], g_ref[...]).astype(BF16)
        kproj = jnp.dot(h, wk_ref[...], preferred_element_type=F32)
        qvt = lax.dot_general(wqvt_ref[...], h, nt, preferred_element_type=F32)
        z = jnp.dot(h, wf_ref[...], preferred_element_type=F32) + bf_ref[...]
        log_f = jnp.minimum(z, 0.0) - jnp.log(1.0 + jnp.exp(-jnp.abs(z)))

        c = carry
        for piece in _split3(log_f):
            c = c + jnp.dot(tri, piece.astype(BF16), preferred_element_type=F32)
        carry = c[sub - 1:sub, :]
        c = c * LOG2E
        c2_ref[0, rows, :] = c
        c_hi, c_mid, c_lo = _split3(c)
        ct_hi, ct_mid, ct_lo = _split3(c.T)

        q_sq = jnp.square(qvt[:d, :] * scale).reshape(n_heads, HEAD_DIM, sub)
        qn = jnp.max(jnp.sum(q_sq, axis=1), axis=1, keepdims=True)
        k_sq = jnp.dot(jnp.square(kproj).astype(BF16), hsel_ref[...], preferred_element_type=F32)
        kn = jnp.max(k_sq, axis=0, keepdims=True)
        qn_max = qn if qn_max is None else jnp.maximum(qn_max, qn)
        kn_max = kn if kn_max is None else jnp.maximum(kn_max, kn)
        for hd in range(n_heads):
            aug_k = jnp.where(lane == 3, -c_hi[:, hd:hd + 1], jnp.where(
                lane == 4, -c_mid[:, hd:hd + 1], jnp.where(
                    lane == 5, -c_lo[:, hd:hd + 1], jnp.where((lane < 3) | (lane == STAB_LANE), 1.0, 0.0))))
            aug_qt = jnp.where(row == 0, ct_hi[hd:hd + 1, :], jnp.where(
                row == 1, ct_mid[hd:hd + 1, :], jnp.where(
                    row == 2, ct_lo[hd:hd + 1, :], jnp.where(row < 6, 1.0, 0.0))))
            c0 = hd * HEAD_DIM
            k_ref[0, hd, rows, :] = jnp.concatenate([kproj[:, c0:c0 + HEAD_DIM], aug_k], axis=-1).astype(BF16)
            qt_ref[0, hd, :, rows] = jnp.concatenate(
                [qvt[c0:c0 + HEAD_DIM, :] * scale, aug_qt], axis=0).astype(BF16)
            vt_ref[0, hd, :, rows] = jnp.concatenate(
                [qvt[d + c0:d + c0 + HEAD_DIM, :], aug_vt], axis=0).astype(BF16)

    carry_ref[...] = carry
    qn_ref[0, 0] = jnp.broadcast_to(qn_max, qn_ref.shape[2:])
    kn_ref[0, 0] = jnp.broadcast_to(kn_max, kn_ref.shape[2:])


def _qkv(x, g, w_k, w_qvt, w_f, b_f, *, tm):
    b, s, d = x.shape
    n_heads = d // HEAD_DIM
    dk = 2 * HEAD_DIM
    sub = min(CHAIN_ROWS, tm)
    tri = jnp.tril(jnp.ones((sub, sub), BF16))
    hsel =(jnp.arange(d)[:, None] // HEAD_DIM == jnp.arange(LANES)[None, :]).astype(BF16)
    kern = functools.partial(_qkv_kernel, tm=tm, n_heads=n_heads)
    row_sds = jax.ShapeDtypeStruct((b, n_heads, s, dk), BF16)
    col_sds = jax.ShapeDtypeStruct((b, n_heads, dk, s), BF16)
    row_spec = pl.BlockSpec((1, n_heads, tm, dk), lambda bi, si: (bi, 0, si, 0))
    col_spec = pl.BlockSpec((1, n_heads, dk, tm), lambda bi, si: (bi, 0, 0, si))
    return pl.pallas_call(
        kern,
        grid=(b, s // tm),
        in_specs=[
            pl.BlockSpec((1, tm, d), lambda bi, si: (bi, si, 0)),
            _resident((1, d)),
            _resident(w_k.shape),
            _resident(w_qvt.shape),
            _resident(w_f.shape),
            _resident(b_f.shape),
            _resident((sub, sub)),
            _resident((d, LANES)),
        ],
        out_specs=[col_spec, row_spec, col_spec,
                   pl.BlockSpec((1, tm, LANES), lambda bi, si: (bi, si, 0)),
                   pl.BlockSpec((1, 1, n_heads, LANES), lambda bi, si: (bi, si, 0, 0)),
                   pl.BlockSpec((1, 1, 8, LANES), lambda bi, si: (bi, si, 0, 0))],
        out_shape=[col_sds, row_sds, col_sds,
                   jax.ShapeDtypeStruct((b, s, LANES), F32),
                   jax.ShapeDtypeStruct((b, s // tm, n_heads, LANES), F32),
                   jax.ShapeDtypeStruct((b, s // tm, 8, LANES), F32)],
        scratch_shapes=[pltpu.VMEM((1, LANES), F32)],
        compiler_params=pltpu.CompilerParams(
            dimension_semantics=("arbitrary", "arbitrary"), vmem_limit_bytes=VMEM_LIMIT),
        name="qkv",
    )(x, g, w_k, w_qvt, w_f, b_f, tri, hsel)


def _attn_kernel(first_ref, stab_ref, perm_ref, *refs, tq, tk, heads_per_step, fixed_stabiliser):
    del perm_ref
    qt_refs = refs[:heads_per_step]
    k_refs = refs[heads_per_step:2 * heads_per_step]
    vt_refs = refs[2 * heads_per_step:3 * heads_per_step]
    eye_ref, o_ref, m_ref, acc_ref, s0_ref, s1_ref = refs[3 * heads_per_step:]
    qi = pl.program_id(2)
    step = (pl.program_id(0) * pl.num_programs(1) + pl.program_id(1)) * pl.num_programs(2) + qi
    first = first_ref[step]
    m_ref[...] = jnp.full(m_ref.shape, NEG_INF, F32)
    acc_ref[...] = jnp.zeros(acc_ref.shape, F32)

    if fixed_stabiliser:
        row = lax.broadcasted_iota(jnp.int32, qt_refs[0].shape[2:], 0)
        head0 = (pl.program_id(0) * pl.num_programs(1) + pl.program_id(1)) * heads_per_step
        qts = [jnp.where(row == HEAD_DIM + STAB_LANE, (-stab_ref[head0 + hh]).astype(BF16), qt_refs[hh][0, 0])
               for hh in range(heads_per_step)]
    else:
        qts = [qt_refs[hh][0, 0] for hh in range(heads_per_step)]

    def scores(sub, dst_ref):
        start = pl.multiple_of(sub * tk, tk)
        for hh in range(heads_per_step):
            k = k_refs[hh][0, 0, pl.ds(start, tk), :]
            dst_ref[hh] = jnp.dot(k, qts[hh], preferred_element_type=F32)

    def consume(sub, src_ref, first_key=None):
        start = pl.multiple_of(sub * tk, tk)
        for hh in range(heads_per_step):
            st = src_ref[hh]
            if first_key is not None:
                key = lax.broadcasted_iota(jnp.int32, (tk, tq), 0) + first_key
                qry = lax.broadcasted_iota(jnp.int32, (tk, tq), 1)
                st = jnp.where(key <= qry, st, NEG_INF)
            vt = vt_refs[hh][0, 0, :, pl.ds(start, tk)]
            if fixed_stabiliser:
                acc_ref[hh] += jnp.dot(vt, jnp.exp2(st).astype(BF16), preferred_element_type=F32)
                continue
            m_old = m_ref[hh]
            m_new = jnp.maximum(m_old, jnp.max(st, axis=0, keepdims=True))
            pt = jnp.exp2(st - m_new).astype(BF16)
            alpha = jnp.exp2(m_old - m_new)
            acc_ref[hh] = alpha * acc_ref[hh] + jnp.dot(vt, pt, preferred_element_type=F32)
            m_ref[hh] = m_new

    subs = tq // tk
    assert subs == 2
    scores(2 * first, s0_ref)

    def trip(j, carry):
        scores(2 * j + 1, s1_ref)
        consume(2 * j, s0_ref)
        scores(2 * j + 2, s0_ref)
        consume(2 * j + 1, s1_ref)
        return carry

    n_blocks = qi - first

    def double_trip(p, carry):
        trip(first + 2 * p, carry)
        return trip(first + 2 * p + 1, carry)

    lax.fori_loop(0, n_blocks // 2, double_trip, 0)

    @pl.when(n_blocks % 2 == 1)
    def _():
        trip(qi - 1, 0)

    scores(2 * qi + 1, s1_ref)
    consume(2 * qi, s0_ref, first_key=0)
    consume(2 * qi + 1, s1_ref, first_key=tk)
    outs = []
    for hh in range(heads_per_step):
        acc = acc_ref[hh]
        outs.append((acc[:HEAD_DIM, :] / acc[HEAD_DIM:HEAD_DIM + 1, :]).astype(o_ref.dtype))
    o_t = jnp.concatenate(outs, axis=0)
    o_ref[0] = lax.dot_general(eye_ref[...], o_t, (((1,), (1,)), ((), ())),
                               preferred_element_type=F32).astype(o_ref.dtype)


UNDERFLOW_LOG2 = 126.0
SKIP_MARGIN_LOG2 = 8.0
STAB_MAX_SPREAD_LOG2 = 100.0


def _first_live_block(c2, qn, kn, *, tq, heads_per_step):
    b, s, _ = c2.shape
    nh = qn.shape[2]
    nq = s // tq
    qk = jnp.sqrt(jnp.max(qn[:, :, :, 0], axis=1) * jnp.max(kn[:, :, 0, :nh], axis=1)) * 1.03
    gap = 2.0 * qk + UNDERFLOW_LOG2 + SKIP_MARGIN_LOG2
    c_query = c2[:, 0::tq, :nh]
    c_key = c2[:, tq - 1::tq, :nh]
    dead = (c_key[:, None, :, :] - c_query[:, :, None, :]) > gap[:, None, None, :]
    dead = dead & (jnp.arange(nq)[None, None, :, None] < jnp.arange(nq)[None, :, None, None])
    first = jnp.sum(dead, axis=2).astype(jnp.int32)
    perm = jnp.argsort(jnp.sum(first, axis=1), axis=-1).astype(jnp.int32)
    first = jnp.take_along_axis(first, perm[:, None, :], axis=2)
    first = jnp.min(first.reshape(b, nq, nh // heads_per_step, heads_per_step), axis=-1)
    stab = jnp.take_along_axis(qk.astype(BF16).astype(F32), perm, axis=1).reshape(-1)
    stab_ok = jnp.max(2.0 * qk) + SKIP_MARGIN_LOG2 < STAB_MAX_SPREAD_LOG2
    return first.transpose(0, 2, 1).reshape(-1), stab, stab_ok, perm


def _attention(qt, k, vt, first, stab, perm, *, tq, fixed_stabiliser):
    b, nh, s, dk = k.shape
    hps = ATTN_HEADS_PER_STEP
    tk = tq // 2
    kern = functools.partial(_attn_kernel, tq=tq, tk=tk, heads_per_step=hps, fixed_stabiliser=fixed_stabiliser)

    def head_map(r, per_q_block, bi, hi, qi, f, m, p):
        head = p[bi * nh + hi * hps + r]
        return (bi, head, 0, qi) if per_q_block else (bi, head, 0, 0)

    return pl.pallas_call(
        kern,
        grid_spec=pltpu.PrefetchScalarGridSpec(
            num_scalar_prefetch=3,
            grid=(b, nh // hps, s // tq),
            in_specs=(
                [pl.BlockSpec((1, 1, dk, tq), functools.partial(head_map, r, True)) for r in range(hps)]
                + [pl.BlockSpec((1, 1, s, dk), functools.partial(head_map, r, False)) for r in range(hps)]
                + [pl.BlockSpec((1, 1, dk, s), functools.partial(head_map, r, False)) for r in range(hps)]
                + [pl.BlockSpec((tq, tq), lambda bi, hi, qi, f, m, p: (0, 0), pipeline_mode=pl.Buffered(1))]),
            out_specs=pl.BlockSpec((1, tq, hps * HEAD_DIM), lambda bi, hi, qi, f, m, p: (bi, qi, hi)),
            scratch_shapes=[pltpu.VMEM((hps, 1, tq), F32), pltpu.VMEM((hps, dk, tq), F32),
                            pltpu.VMEM((hps, tk, tq), F32), pltpu.VMEM((hps, tk, tq), F32)],
        ),
        out_shape=jax.ShapeDtypeStruct((b, s, nh * HEAD_DIM), BF16),
        compiler_params=pltpu.CompilerParams(
            dimension_semantics=("arbitrary", "arbitrary", "arbitrary"), vmem_limit_bytes=VMEM_LIMIT),
        name="attention",
    )(first, stab, perm.reshape(-1), *([qt] * hps), *([k] * hps), *([vt] * hps), jnp.eye(tq, dtype=BF16))


def _attn_out_kernel(o_ref, x_ref, g_ref, wo_ref, wrh_ref, wrl_ref, ltri_ref, x3_ref, ri_ref, rw_ref, cnt_ref,
                     carry_ref, *, tm, sub):
    i = pl.program_id(0)
    g = g_ref[...]

    @pl.when(i == 0)
    def _():
        carry_ref[...] = jnp.zeros_like(carry_ref)

    carry = carry_ref[...]
    lane = lax.broadcasted_iota(jnp.int32, (sub, LANES), 1)
    for r0 in range(0, tm, sub):
        rows = pl.ds(r0, sub)
        y = jnp.dot(o_ref[rows, :], wo_ref[0], preferred_element_type=F32)
        x3 = x_ref[rows, :] + _rms(y, g[0:1])
        x3_ref[rows, :] = x3
        h = _rms(x3, g[1:2])
        h_hi = h.astype(BF16)
        h_lo = (h - h_hi.astype(F32)).astype(BF16)
        logits = (jnp.dot(h_hi, wrh_ref[...], preferred_element_type=F32)
                  + jnp.dot(h_lo, wrh_ref[...], preferred_element_type=F32)
                  + jnp.dot(h_hi, wrl_ref[...], preferred_element_type=F32))
        logits = jnp.where(lane < N_EXPERTS, logits, -jnp.inf)
        m1 = jnp.max(logits, axis=-1, keepdims=True)
        e1 = jnp.min(jnp.where(logits == m1, lane, LANES), axis=-1, keepdims=True)
        rest = jnp.where(lane == e1, -jnp.inf, logits)
        m2 = jnp.max(rest, axis=-1, keepdims=True)
        e2 = jnp.min(jnp.where(rest == m2, lane, LANES), axis=-1, keepdims=True)
        t = jnp.exp(m2 - m1)
        w1 = 1.0 / (1.0 + t)
        w2 = t / (1.0 + t)

        hot1 = lane == e1
        hot2 = lane == e2
        cnt = jnp.where(hot1 | hot2, 1.0, 0.0)
        before = jnp.dot(ltri_ref[...], cnt.astype(BF16), preferred_element_type=F32) + carry
        r1 = jnp.sum(jnp.where(hot1, before, 0.0), axis=-1, keepdims=True)
        r2 = jnp.sum(jnp.where(hot2, before, 0.0), axis=-1, keepdims=True)
        carry = carry + jnp.sum(cnt, axis=0, keepdims=True)
        ri = jnp.where(lane == 0, e1, jnp.where(lane == 1, e2, jnp.where(
            lane == 2, r1.astype(jnp.int32), jnp.where(lane == 3, r2.astype(jnp.int32), 0))))
        ri_ref[:, rows] = ri.T[:META_LANES, :]
        rw_ref[rows, :] = jnp.where(lane == 0, w1, jnp.where(lane == 1, w2, 0.0))[:, :META_LANES]

    carry_ref[...] = carry
    cnt_ref[...] = jnp.broadcast_to(carry, cnt_ref.shape).astype(jnp.int32)


def _attn_out(o, x, g2, w_out, w_router_pad, *, tm, tiles_per_batch):
    n, d = x.shape
    sub = min(CHAIN_ROWS, tm)
    ltri = jnp.tril(jnp.ones((sub, sub), BF16), k=-1)
    wr_hi = w_router_pad.astype(BF16)
    wr_lo = (w_router_pad - wr_hi.astype(F32)).astype(BF16)
    kern = functools.partial(_attn_out_kernel, tm=tm, sub=sub)
    row_spec = pl.BlockSpec((tm, d), lambda i: (i, 0))
    meta_spec = pl.BlockSpec((tm, META_LANES), lambda i: (i, 0))
    return pl.pallas_call(
        kern,
        grid=(n // tm,),
        in_specs=[row_spec, row_spec, _resident((2, d)),
                  pl.BlockSpec((1, d, d), lambda i: (i // tiles_per_batch, 0, 0)),
                  _resident(wr_hi.shape), _resident(wr_lo.shape), _resident((sub, sub))],
        out_specs=[row_spec, pl.BlockSpec((META_LANES, tm), lambda i: (0, i)), meta_spec,
                   pl.BlockSpec((8, LANES), lambda i: (0, 0))],
        out_shape=[jax.ShapeDtypeStruct((n, d), F32),
                   jax.ShapeDtypeStruct((META_LANES, n), jnp.int32),
                   jax.ShapeDtypeStruct((n, META_LANES), F32),
                   jax.ShapeDtypeStruct((8, LANES), jnp.int32)],
        scratch_shapes=[pltpu.VMEM((1, LANES), F32)],
        compiler_params=pltpu.CompilerParams(
            dimension_semantics=("arbitrary",), vmem_limit_bytes=VMEM_LIMIT),
        name="attn_out",
    )(o, x, g2, w_out, wr_hi, wr_lo, ltri)


def _row_copy(src, dst, sem):
    return pltpu.make_async_copy(src, dst, sem)


def _dispatch_kernel(fill_ref, pos_ref, x_ref, g_ref, xs_ref, h_buf, z_buf, sem, zsem, *, tb):
    i = pl.program_id(0)
    last = pl.num_programs(0) - 1
    slot = lax.rem(i, 2)
    h_buf[slot] = _rms(x_ref[...], g_ref[...])

    def issue(t, _):
        for which in range(2):
            p = pos_ref[0, which, t]
            _row_copy(h_buf.at[slot, pl.ds(t, 1), :], xs_ref.at[pl.ds(p, 1), :], sem.at[slot]).start()
        return 0

    lax.fori_loop(0, tb, issue, 0, unroll=ISSUE_UNROLL)

    def drain(s):
        for _ in range(2 * tb):
            _row_copy(h_buf.at[s, pl.ds(0, 1), :], xs_ref.at[pl.ds(0, 1), :], sem.at[s]).wait()

    @pl.when(i == 0)
    def _():
        z_buf[...] = jnp.zeros_like(z_buf)
        for e in range(N_EXPERTS):
            lo = fill_ref[0, e]
            hi = fill_ref[1, e]

            def fill(r, _):
                _row_copy(z_buf, xs_ref.at[pl.ds(r, 1), :], zsem).start()
                return 0

            lax.fori_loop(lo, hi, fill, 0)

            def fill_done(r, _):
                _row_copy(z_buf, xs_ref.at[pl.ds(0, 1), :], zsem).wait()
                return 0

            lax.fori_loop(lo, hi, fill_done, 0)

    @pl.when(i > 0)
    def _():
        drain(1 - slot)

    @pl.when(i == last)
    def _():
        drain(slot)


def _dispatch(x3, g, pos, fill, n_rows, *, tb):
    n, d = x3.shape
    kern = functools.partial(_dispatch_kernel, tb=tb)
    return pl.pallas_call(
        kern,
        grid_spec=pltpu.PrefetchScalarGridSpec(
            num_scalar_prefetch=1,
            grid=(n // tb,),
            in_specs=[
                pl.BlockSpec((1, 2, tb), lambda i, f: (i, 0, 0), memory_space=pltpu.SMEM),
                pl.BlockSpec((tb, d), lambda i, f: (i, 0)),
                pl.BlockSpec((1, d), lambda i, f: (0, 0)),
            ],
            out_specs=pl.BlockSpec(memory_space=pl.ANY),
            scratch_shapes=[pltpu.VMEM((2, tb, d), F32), pltpu.VMEM((1, d), F32),
                            pltpu.SemaphoreType.DMA((2,)), pltpu.SemaphoreType.DMA(())],
        ),
        out_shape=jax.ShapeDtypeStruct((n_rows, d), F32),
        compiler_params=pltpu.CompilerParams(
            dimension_semantics=("arbitrary",), vmem_limit_bytes=VMEM_LIMIT),
        name="dispatch",
    )(fill, pos, x3, g)


def _experts_kernel(te_ref, nt_ref, xs_ref, wg_ref, wu_ref, wd_ref, ys_ref, xb_ref, acc_ref, *, nj):
    i = pl.program_id(0)
    j = pl.program_id(1)
    live = i < nt_ref[0]
    tm = xb_ref.shape[0]
    sub = min(CHAIN_ROWS, tm)

    def partial_out(xb):
        gate = jnp.dot(xb, wg_ref[0], preferred_element_type=F32)
        up = jnp.dot(xb, wu_ref[0], preferred_element_type=F32)
        act = (gate * jax.nn.sigmoid(gate) * up).astype(BF16)
        return jnp.dot(act, wd_ref[0], preferred_element_type=F32)

    @pl.when(live & (j == 0))
    def _():
        for r0 in range(0, tm, sub):
            xb = xs_ref[r0:r0 + sub, :].astype(BF16)
            if nj > 1:
                xb_ref[r0:r0 + sub, :] = xb
                acc_ref[r0:r0 + sub, :] = partial_out(xb)
            else:
                ys_ref[r0:r0 + sub, :] = partial_out(xb)

    if nj > 2:
        @pl.when(live & (j > 0) & (j < nj - 1))
        def _():
            for r0 in range(0, tm, sub):
                acc_ref[r0:r0 + sub, :] += partial_out(xb_ref[r0:r0 + sub, :])

    if nj > 1:
        @pl.when(live & (j == nj - 1))
        def _():
            for r0 in range(0, tm, sub):
                ys_ref[r0:r0 + sub, :] = acc_ref[r0:r0 + sub, :] + partial_out(xb_ref[r0:r0 + sub, :])

    @pl.when(jnp.logical_not(live) & (j == nj - 1))
    def _():
        ys_ref[...] = jnp.zeros_like(ys_ref)


def _experts(xs, tile_expert, n_tiles, w_gate, w_up, w_down, *, tm, tf):
    n_rows, d = xs.shape
    dff = w_gate.shape[-1]
    max_tiles = n_rows // tm

    def row_map(i, j, te, nt):
        return (jnp.minimum(i, nt[0] - 1), 0)

    def wcol_map(i, j, te, nt):
        return (te[i], 0, j)

    def wrow_map(i, j, te, nt):
        return (te[i], j, 0)

    return pl.pallas_call(
        functools.partial(_experts_kernel, nj=dff // tf),
        grid_spec=pltpu.PrefetchScalarGridSpec(
            num_scalar_prefetch=2,
            grid=(max_tiles, dff // tf),
            in_specs=[
                pl.BlockSpec((tm, d), row_map),
                pl.BlockSpec((1, d, tf), wcol_map),
                pl.BlockSpec((1, d, tf), wcol_map),
                pl.BlockSpec((1, tf, d), wrow_map),
            ],
            out_specs=pl.BlockSpec((tm, d), lambda i, j, te, nt: (i, 0)),
            scratch_shapes=[pltpu.VMEM((tm, d), BF16), pltpu.VMEM((tm, d), F32)],
        ),
        out_shape=jax.ShapeDtypeStruct((n_rows, d), F32),
        compiler_params=pltpu.CompilerParams(
            dimension_semantics=("arbitrary", "arbitrary"), vmem_limit_bytes=VMEM_LIMIT),
        name="experts",
    )(tile_expert, n_tiles, xs, w_gate, w_up, w_down)


def _combine_kernel(pos_ref, posn_ref, ys_ref, x_ref, rw_ref, g_ref, o_ref, a_buf, b_buf, sem, *, tb):
    i = pl.program_id(0)
    last = pl.num_programs(0) - 1
    slot = lax.rem(i, 2)

    def gather(idx_ref, s):
        def issue(t, _):
            _row_copy(ys_ref.at[pl.ds(idx_ref[0, 0, t], 1), :], a_buf.at[s, pl.ds(t, 1), :], sem.at[s]).start()
            _row_copy(ys_ref.at[pl.ds(idx_ref[0, 1, t], 1), :], b_buf.at[s, pl.ds(t, 1), :], sem.at[s]).start()
            return 0

        lax.fori_loop(0, tb, issue, 0, unroll=ISSUE_UNROLL)

    @pl.when(i == 0)
    def _():
        gather(pos_ref, slot)

    @pl.when(i < last)
    def _():
        gather(posn_ref, 1 - slot)

    for _ in range(2 * tb):
        _row_copy(ys_ref.at[pl.ds(0, 1), :], a_buf.at[slot, pl.ds(0, 1), :], sem.at[slot]).wait()
    rw = rw_ref[...]
    y = rw[:, 0:1] * a_buf[slot] + rw[:, 1:2] * b_buf[slot]
    o_ref[...] = x_ref[...] + _rms(y, g_ref[...])


def _combine(ys, x3, rw, g, pos, *, tb):
    n, d = x3.shape
    n_tiles = n // tb
    kern = functools.partial(_combine_kernel, tb=tb)
    return pl.pallas_call(
        kern,
        grid=(n_tiles,),
        in_specs=[
            pl.BlockSpec((1, 2, tb), lambda i: (i, 0, 0), memory_space=pltpu.SMEM),
            pl.BlockSpec((1, 2, tb), lambda i: (jnp.minimum(i + 1, n_tiles - 1), 0, 0), memory_space=pltpu.SMEM),
            pl.BlockSpec(memory_space=pl.ANY),
            pl.BlockSpec((tb, d), lambda i: (i, 0)),
            pl.BlockSpec((tb, META_LANES), lambda i: (i, 0)),
            pl.BlockSpec((1, d), lambda i: (0, 0)),
        ],
        out_specs=pl.BlockSpec((tb, d), lambda i: (i, 0)),
        out_shape=jax.ShapeDtypeStruct((n, d), F32),
        scratch_shapes=[pltpu.VMEM((2, tb, d), F32), pltpu.VMEM((2, tb, d), F32),
                        pltpu.SemaphoreType.DMA((2,))],
        compiler_params=pltpu.CompilerParams(
            dimension_semantics=("arbitrary",), vmem_limit_bytes=VMEM_LIMIT),
        name="combine",
    )(pos, pos, ys, x3, rw, g)


def _pick_tile(n, pref):
    t = min(pref, n)
    while n % t:
        t //= 2
    return t


def kernel(x, norm_g, pool_w, pool_scale, attn_w_in, attn_b_f, attn_w_out, ffn_w_gate, ffn_w_up, ffn_w_down,
           moe_w_router, moe_w_gate, moe_w_up, moe_w_down):
    b, s, d = x.shape
    n = b * s
    nh = d // HEAD_DIM
    tq = _pick_tile(s, 512)
    tm = _pick_tile(s, 512)
    tm_qkv = _pick_tile(s, 1024)

    x = _layer0(x, norm_g[0], pool_w[0].astype(BF16), pool_scale[0],
                ffn_w_gate[0].astype(BF16), ffn_w_up[0].astype(BF16), ffn_w_down[0].astype(BF16), tm=tm)

    w_in = attn_w_in[0]
    w_f = jnp.zeros((d, LANES), F32).at[:, :nh].set(w_in[:, 3 * d:]).astype(BF16)
    b_f = jnp.zeros((1, LANES), F32).at[0, :nh].set(attn_b_f[0])
    w_qvt = jnp.concatenate([w_in[:, :d], w_in[:, 2 * d:3 * d]], axis=1).T.astype(BF16)
    qt, k, vt, c2, qn, kn = _qkv(x, norm_g[1, 0:1], w_in[:, d:2 * d].astype(BF16), w_qvt, w_f, b_f, tm=tm_qkv)
    first, stab, stab_ok, perm = _first_live_block(c2, qn, kn, tq=tq, heads_per_step=ATTN_HEADS_PER_STEP)
    o = lax.cond(stab_ok,
                 functools.partial(_attention, tq=tq, fixed_stabiliser=True),
                 functools.partial(_attention, tq=tq, fixed_stabiliser=False),
                 qt, k, vt, first, stab, perm)
    w_out = attn_w_out[0].astype(BF16).reshape(nh, HEAD_DIM, d)[perm].reshape(b, d, d)

    w_router = jnp.zeros((d, LANES), F32).at[:, :N_EXPERTS].set(moe_w_router[0])
    x3, ri, rw, counts = _attn_out(o.reshape(n, d), x.reshape(n, d), norm_g[1, 1:3], w_out, w_router,
                                   tm=tm, tiles_per_batch=s // tm)

    tme = _pick_tile(n, 512)
    counts = counts[0, :N_EXPERTS]
    padded = (counts + tme - 1) // tme * tme
    ends = jnp.cumsum(padded)
    starts = ends - padded
    n_rows = 2 * n + N_EXPERTS * tme
    pos = jnp.stack([starts[ri[0]] + ri[2], starts[ri[1]] + ri[3]], axis=0)
    tb = _pick_tile(n, 256)
    pos = pos.reshape(2, n // tb, tb).transpose(1, 0, 2)
    fill_hi = ends.at[N_EXPERTS - 1].set(n_rows)
    fill = jnp.stack([starts + counts, fill_hi], axis=0).astype(jnp.int32)
    tile_start = jnp.arange(n_rows // tme, dtype=jnp.int32) * tme
    tile_expert = jnp.minimum(jnp.sum(tile_start[:, None] >= ends[None, :], axis=1), N_EXPERTS - 1).astype(jnp.int32)
    n_tiles = (ends[-1:] // tme).astype(jnp.int32)

    xs = _dispatch(x3, norm_g[1, 2:3], pos, fill, n_rows, tb=tb)
    dffe = moe_w_gate.shape[-1]
    tf = dffe // 2 if (dffe // 2) % LANES == 0 else dffe
    ys = _experts(xs, tile_expert, n_tiles, moe_w_gate[0].astype(BF16), moe_w_up[0].astype(BF16),
                  moe_w_down[0].astype(BF16), tm=tme, tf=tf)
    out = _combine(ys, x3, rw, norm_g[1, 3:4], pos, tb=tb)
    return out.reshape(b, s, d)
```

```python
import functools

import jax
import jax.numpy as jnp
from jax import lax
from jax.experimental import pallas as pl
from jax.experimental.pallas import tpu as pltpu

F32 = jnp.float32
BF16 = jnp.bfloat16

RMS_EPS = 1e-6
HEAD_DIM = 64
POOL_WINDOWS = (2, 4, 8, 16)
POOL_HALO = 16
N_EXPERTS = 8
NEG_INF = -1e30
LANES = 128
VMEM_LIMIT = 56 * 1024 * 1024
ISSUE_UNROLL = 8
LOG2E = 1.4426950408889634
ATTN_HEADS_PER_STEP = 4
CHAIN_ROWS = 256
META_LANES = 8
STAB_LANE = 6


def _rms(x, g):
    ms = jnp.mean(x * x, axis=-1, keepdims=True)
    return x * lax.rsqrt(ms + RMS_EPS) * g


def _split3(c):
    hi = c.astype(BF16).astype(F32)
    r = c - hi
    mid = r.astype(BF16).astype(F32)
    lo = r - mid
    return hi, mid, lo


def _resident(shape):
    nd = len(shape)
    return pl.BlockSpec(shape, lambda *_: (0,) * nd, pipeline_mode=pl.Buffered(1))


def _layer0_kernel(x_ref, xp_ref, g_ref, pw_ref, ps_ref, wg_ref, wu_ref, wd_ref, o_ref,
                   buf_a, buf_b, *, tm, sub, ff_chunk):
    s = pl.program_id(1)
    d = x_ref.shape[-1]
    gd = d // len(POOL_WINDOWS)
    g = g_ref[...]
    dff = wg_ref.shape[1]
    lo = 8
    top = sub + POOL_HALO + lo
    body = lo + POOL_HALO
    zeros8 = jnp.zeros((lo, d), F32)
    halo = jnp.where(s > 0, _rms(xp_ref[0], g[0:1]), 0.0)

    for idx, r0 in enumerate(range(0, tm, sub)):
        x = x_ref[0, r0:r0 + sub, :]
        h = _rms(x, g[0:1])
        ba = buf_a.at[idx]
        bb = buf_b.at[idx]
        ba[0:lo, :] = zeros8
        bb[0:lo, :] = zeros8
        ba[lo:body, :] = halo
        ba[body:top, :] = h
        halo = h[sub - POOL_HALO:, :]

        def shifted_sum(src, k, c0):
            return src[lo:top, c0:] + src[lo - k:top - k, c0:]

        sums = [ba[body:top, 0:gd] + ba[body - 1:top - 1, 0:gd]]
        bb[lo:top, gd:] = shifted_sum(ba, 1, gd)
        sums.append(bb[body:top, gd:2 * gd] + bb[body - 2:top - 2, gd:2 * gd])
        ba[lo:top, 2 * gd:] = shifted_sum(bb, 2, 2 * gd)
        sums.append(ba[body:top, 2 * gd:3 * gd] + ba[body - 4:top - 4, 2 * gd:3 * gd])
        bb[lo:top, 3 * gd:] = shifted_sum(ba, 4, 3 * gd)
        sums.append(bb[body:top, 3 * gd:] + bb[body - 8:top - 8, 3 * gd:])

        pos = s * tm + r0 + lax.broadcasted_iota(jnp.int32, (sub, 1), 0)
        mixed = []
        for gi, w in enumerate(POOL_WINDOWS):
            count = jnp.minimum(pos + 1, w).astype(F32)
            pooled = sums[gi] / count - h[:, gi * gd:(gi + 1) * gd]
            mixed.append(jnp.dot(pooled.astype(BF16), pw_ref[gi], preferred_element_type=F32))
        y = jnp.concatenate(mixed, axis=-1) * ps_ref[...]
        x1 = x + _rms(y, g[1:2])

        h2 = _rms(x1, g[2:3]).astype(BF16)
        acc = jnp.zeros((sub, d), F32)
        for c0 in range(0, dff, ff_chunk):
            gate = jnp.dot(h2, wg_ref[:, c0:c0 + ff_chunk], preferred_element_type=F32)
            up = jnp.dot(h2, wu_ref[:, c0:c0 + ff_chunk], preferred_element_type=F32)
            act = (gate * jax.nn.sigmoid(gate) * up).astype(BF16)
            acc = acc + jnp.dot(act, wd_ref[c0:c0 + ff_chunk, :], preferred_element_type=F32)
        o_ref[0, r0:r0 + sub, :] = x1 + _rms(acc, g[3:4])


def _layer0(x, g4, pool_w, pool_scale, w_gate, w_up, w_down, *, tm):
    b, s, d = x.shape
    dff = w_gate.shape[1]
    ff_chunk = dff
    halo_per_tile = tm // POOL_HALO
    sub = min(CHAIN_ROWS, tm)
    kern = functools.partial(_layer0_kernel, tm=tm, sub=sub, ff_chunk=ff_chunk)
    return pl.pallas_call(
        kern,
        grid=(b, s // tm),
        in_specs=[
            pl.BlockSpec((1, tm, d), lambda bi, si: (bi, si, 0)),
            pl.BlockSpec((1, POOL_HALO, d), lambda bi, si: (bi, jnp.maximum(si * halo_per_tile - 1, 0), 0)),
            _resident((4, d)),
            _resident(pool_w.shape),
            _resident((1, d)),
            _resident(w_gate.shape),
            _resident(w_up.shape),
            _resident(w_down.shape),
        ],
        out_specs=pl.BlockSpec((1, tm, d), lambda bi, si: (bi, si, 0)),
        out_shape=jax.ShapeDtypeStruct((b, s, d), F32),
        scratch_shapes=[pltpu.VMEM((tm // sub, sub + POOL_HALO + 8, d), F32),
                        pltpu.VMEM((tm // sub, sub + POOL_HALO + 8, d), F32)],
        compiler_params=pltpu.CompilerParams(
            dimension_semantics=("arbitrary", "arbitrary"), vmem_limit_bytes=VMEM_LIMIT),
        name="layer0",
    )(x, x, g4, pool_w, pool_scale.reshape(1, d), w_gate, w_up, w_down)


def _qkv_kernel(x_ref, g_ref, wk_ref, wqvt_ref, wf_ref, bf_ref, tri_ref, hsel_ref,
                qt_ref, k_ref, vt_ref, c2_ref, qn_ref, kn_ref, carry_ref, *, tm, n_heads):
    s = pl.program_id(1)
    d = x_ref.shape[-1]
    nt = (((1,), (1,)), ((), ()))

    @pl.when(s == 0)
    def _():
        carry_ref[...] = jnp.zeros_like(carry_ref)

    sub = min(CHAIN_ROWS, tm)
    lane = lax.broadcasted_iota(jnp.int32, (sub, HEAD_DIM), 1)
    row = lax.broadcasted_iota(jnp.int32, (HEAD_DIM, sub), 0)
    aug_vt = jnp.where(row == 0, 1.0, 0.0)
    scale = HEAD_DIM ** -0.5 * LOG2E
    tri = tri_ref[...]
    carry = carry_ref[...]
    qn_max = None
    kn_max = None
    for r0 in range(0, tm, sub):
        rows = pl.ds(r0, sub)
        h = _rms(x_ref[0, rows, :], g_ref[...]).astype(BF16)
        kproj = jnp.dot(h, wk_ref[...], preferred_element_type=F32)
        qvt = lax.dot_general(wqvt_ref[...], h, nt, preferred_element_type=F32)
        z = jnp.dot(h, wf_ref[...], preferred_element_type=F32) + bf_ref[...]
        log_f = jnp.minimum(z, 0.0) - jnp.log(1.0 + jnp.exp(-jnp.abs(z)))

        c = carry
        for piece in _split3(log_f):
            c = c + jnp.dot(tri, piece.astype(BF16), preferred_element_type=F32)
        carry = c[sub - 1:sub, :]
        c = c * LOG2E
        c2_ref[0, rows, :] = c
        c_hi, c_mid, c_lo = _split3(c)
        ct_hi, ct_mid, ct_lo = _split3(c.T)

        q_sq = jnp.square(qvt[:d, :] * scale).reshape(n_heads, HEAD_DIM, sub)
        qn = jnp.max(jnp.sum(q_sq, axis=1), axis=1, keepdims=True)
        k_sq = jnp.dot(jnp.square(kproj).astype(BF16), hsel_ref[...], preferred_element_type=F32)
        kn = jnp.max(k_sq, axis=0, keepdims=True)
        qn_max = qn if qn_max is None else jnp.maximum(qn_max, qn)
        kn_max = kn if kn_max is None else jnp.maximum(kn_max, kn)
        for hd in range(n_heads):
            aug_k = jnp.where(lane == 3, -c_hi[:, hd:hd + 1], jnp.where(
                lane == 4, -c_mid[:, hd:hd + 1], jnp.where(
                    lane == 5, -c_lo[:, hd:hd + 1], jnp.where((lane < 3) | (lane == STAB_LANE), 1.0, 0.0))))
            aug_qt = jnp.where(row == 0, ct_hi[hd:hd + 1, :], jnp.where(
                row == 1, ct_mid[hd:hd + 1, :], jnp.where(
                    row == 2, ct_lo[hd:hd + 1, :], jnp.where(row < 6, 1.0, 0.0))))
            c0 = hd * HEAD_DIM
            k_ref[0, hd, rows, :] = jnp.concatenate([kproj[:, c0:c0 + HEAD_DIM], aug_k], axis=-1).astype(BF16)
            qt_ref[0, hd, :, rows] = jnp.concatenate(
                [qvt[c0:c0 + HEAD_DIM, :] * scale, aug_qt], axis=0).astype(BF16)
            vt_ref[0, hd, :, rows] = jnp.concatenate(
                [qvt[d + c0:d + c0 + HEAD_DIM, :], aug_vt], axis=0).astype(BF16)

    carry_ref[...] = carry
    qn_ref[0, 0] = jnp.broadcast_to(qn_max, qn_ref.shape[2:])
    kn_ref[0, 0] = jnp.broadcast_to(kn_max, kn_ref.shape[2:])


def _qkv(x, g, w_k, w_qvt, w_f, b_f, *, tm):
    b, s, d = x.shape
    n_heads = d // HEAD_DIM
    dk = 2 * HEAD_DIM
    sub = min(CHAIN_ROWS, tm)
    tri = jnp.tril(jnp.ones((sub, sub), BF16))
    hsel =(jnp.arange(d)[:, None] // HEAD_DIM == jnp.arange(LANES)[None, :]).astype(BF16)
    kern = functools.partial(_qkv_kernel, tm=tm, n_heads=n_heads)
    row_sds = jax.ShapeDtypeStruct((b, n_heads, s, dk), BF16)
    col_sds = jax.ShapeDtypeStruct((b, n_heads, dk, s), BF16)
    row_spec = pl.BlockSpec((1, n_heads, tm, dk), lambda bi, si: (bi, 0, si, 0))
    col_spec = pl.BlockSpec((1, n_heads, dk, tm), lambda bi, si: (bi, 0, 0, si))
    return pl.pallas_call(
        kern,
        grid=(b, s // tm),
        in_specs=[
            pl.BlockSpec((1, tm, d), lambda bi, si: (bi, si, 0)),
            _resident((1, d)),
            _resident(w_k.shape),
            _resident(w_qvt.shape),
            _resident(w_f.shape),
            _resident(b_f.shape),
            _resident((sub, sub)),
            _resident((d, LANES)),
        ],
        out_specs=[col_spec, row_spec, col_spec,
                   pl.BlockSpec((1, tm, LANES), lambda bi, si: (bi, si, 0)),
                   pl.BlockSpec((1, 1, n_heads, LANES), lambda bi, si: (bi, si, 0, 0)),
                   pl.BlockSpec((1, 1, 8, LANES), lambda bi, si: (bi, si, 0, 0))],
        out_shape=[col_sds, row_sds, col_sds,
                   jax.ShapeDtypeStruct((b, s, LANES), F32),
                   jax.ShapeDtypeStruct((b, s // tm, n_heads, LANES), F32),
                   jax.ShapeDtypeStruct((b, s // tm, 8, LANES), F32)],
        scratch_shapes=[pltpu.VMEM((1, LANES), F32)],
        compiler_params=pltpu.CompilerParams(
            dimension_semantics=("arbitrary", "arbitrary"), vmem_limit_bytes=VMEM_LIMIT),
        name="qkv",
    )(x, g, w_k, w_qvt, w_f, b_f, tri, hsel)


def _attn_kernel(first_ref, stab_ref, perm_ref, *refs, tq, tk, heads_per_step, fixed_stabiliser):
    del perm_ref
    qt_refs = refs[:heads_per_step]
    k_refs = refs[heads_per_step:2 * heads_per_step]
    vt_refs = refs[2 * heads_per_step:3 * heads_per_step]
    eye_ref, o_ref, m_ref, acc_ref, s0_ref, s1_ref = refs[3 * heads_per_step:]
    qi = pl.program_id(2)
    step = (pl.program_id(0) * pl.num_programs(1) + pl.program_id(1)) * pl.num_programs(2) + qi
    first = first_ref[step]
    m_ref[...] = jnp.full(m_ref.shape, NEG_INF, F32)
    acc_ref[...] = jnp.zeros(acc_ref.shape, F32)

    if fixed_stabiliser:
        row = lax.broadcasted_iota(jnp.int32, qt_refs[0].shape[2:], 0)
        head0 = (pl.program_id(0) * pl.num_programs(1) + pl.program_id(1)) * heads_per_step
        qts = [jnp.where(row == HEAD_DIM + STAB_LANE, (-stab_ref[head0 + hh]).astype(BF16), qt_refs[hh][0, 0])
               for hh in range(heads_per_step)]
    else:
        qts = [qt_refs[hh][0, 0] for hh in range(heads_per_step)]

    def scores(sub, dst_ref):
        start = pl.multiple_of(sub * tk, tk)
        for hh in range(heads_per_step):
            k = k_refs[hh][0, 0, pl.ds(start, tk), :]
            dst_ref[hh] = jnp.dot(k, qts[hh], preferred_element_type=F32)

    def consume(sub, src_ref, first_key=None):
        start = pl.multiple_of(sub * tk, tk)
        for hh in range(heads_per_step):
            st = src_ref[hh]
            if first_key is not None:
                key = lax.broadcasted_iota(jnp.int32, (tk, tq), 0) + first_key
                qry = lax.broadcasted_iota(jnp.int32, (tk, tq), 1)
                st = jnp.where(key <= qry, st, NEG_INF)
            vt = vt_refs[hh][0, 0, :, pl.ds(start, tk)]
            if fixed_stabiliser:
                acc_ref[hh] += jnp.dot(vt, jnp.exp2(st).astype(BF16), preferred_element_type=F32)
                continue
            m_old = m_ref[hh]
            m_new = jnp.maximum(m_old, jnp.max(st, axis=0, keepdims=True))
            pt = jnp.exp2(st - m_new).astype(BF16)
            alpha = jnp.exp2(m_old - m_new)
            acc_ref[hh] = alpha * acc_ref[hh] + jnp.dot(vt, pt, preferred_element_type=F32)
            m_ref[hh] = m_new

    subs = tq // tk
    assert subs == 2
    scores(2 * first, s0_ref)

    def trip(j, carry):
        scores(2 * j + 1, s1_ref)
        consume(2 * j, s0_ref)
        scores(2 * j + 2, s0_ref)
        consume(2 * j + 1, s1_ref)
        return carry

    n_blocks = qi - first

    def double_trip(p, carry):
        trip(first + 2 * p, carry)
        return trip(first + 2 * p + 1, carry)

    lax.fori_loop(0, n_blocks // 2, double_trip, 0)

    @pl.when(n_blocks % 2 == 1)
    def _():
        trip(qi - 1, 0)

    scores(2 * qi + 1, s1_ref)
    consume(2 * qi, s0_ref, first_key=0)
    consume(2 * qi + 1, s1_ref, first_key=tk)
    outs = []
    for hh in range(heads_per_step):
        acc = acc_ref[hh]
        outs.append((acc[:HEAD_DIM, :] / acc[HEAD_DIM:HEAD_DIM + 1, :]).astype(o_ref.dtype))
    o_t = jnp.concatenate(outs, axis=0)
    o_ref[0] = lax.dot_general(eye_ref[...], o_t, (((1,), (1,)), ((), ())),
                               preferred_element_type=F32).astype(o_ref.dtype)


UNDERFLOW_LOG2 = 126.0
SKIP_MARGIN_LOG2 = 8.0
STAB_MAX_SPREAD_LOG2 = 100.0


def _first_live_block(c2, qn, kn, *, tq, heads_per_step):
    b, s, _ = c2.shape
    nh = qn.shape[2]
    nq = s // tq
    qk = jnp.sqrt(jnp.max(qn[:, :, :, 0], axis=1) * jnp.max(kn[:, :, 0, :nh], axis=1)) * 1.03
    gap = 2.0 * qk + UNDERFLOW_LOG2 + SKIP_MARGIN_LOG2
    c_query = c2[:, 0::tq, :nh]
    c_key = c2[:, tq - 1::tq, :nh]
    dead = (c_key[:, None, :, :] - c_query[:, :, None, :]) > gap[:, None, None, :]
    dead = dead & (jnp.arange(nq)[None, None, :, None] < jnp.arange(nq)[None, :, None, None])
    first = jnp.sum(dead, axis=2).astype(jnp.int32)
    perm = jnp.argsort(jnp.sum(first, axis=1), axis=-1).astype(jnp.int32)
    first = jnp.take_along_axis(first, perm[:, None, :], axis=2)
    first = jnp.min(first.reshape(b, nq, nh // heads_per_step, heads_per_step), axis=-1)
    stab = jnp.take_along_axis(qk.astype(BF16).astype(F32), perm, axis=1).reshape(-1)
    stab_ok = jnp.max(2.0 * qk) + SKIP_MARGIN_LOG2 < STAB_MAX_SPREAD_LOG2
    return first.transpose(0, 2, 1).reshape(-1), stab, stab_ok, perm


def _attention(qt, k, vt, first, stab, perm, *, tq, fixed_stabiliser):
    b, nh, s, dk = k.shape
    hps = ATTN_HEADS_PER_STEP
    tk = tq // 2
    kern = functools.partial(_attn_kernel, tq=tq, tk=tk, heads_per_step=hps, fixed_stabiliser=fixed_stabiliser)

    def head_map(r, per_q_block, bi, hi, qi, f, m, p):
        head = p[bi * nh + hi * hps + r]
        return (bi, head, 0, qi) if per_q_block else (bi, head, 0, 0)

    return pl.pallas_call(
        kern,
        grid_spec=pltpu.PrefetchScalarGridSpec(
            num_scalar_prefetch=3,
            grid=(b, nh // hps, s // tq),
            in_specs=(
                [pl.BlockSpec((1, 1, dk, tq), functools.partial(head_map, r, True)) for r in range(hps)]
                + [pl.BlockSpec((1, 1, s, dk), functools.partial(head_map, r, False)) for r in range(hps)]
                + [pl.BlockSpec((1, 1, dk, s), functools.partial(head_map, r, False)) for r in range(hps)]
                + [pl.BlockSpec((tq, tq), lambda bi, hi, qi, f, m, p: (0, 0), pipeline_mode=pl.Buffered(1))]),
            out_specs=pl.BlockSpec((1, tq, hps * HEAD_DIM), lambda bi, hi, qi, f, m, p: (bi, qi, hi)),
            scratch_shapes=[pltpu.VMEM((hps, 1, tq), F32), pltpu.VMEM((hps, dk, tq), F32),
                            pltpu.VMEM((hps, tk, tq), F32), pltpu.VMEM((hps, tk, tq), F32)],
        ),
        out_shape=jax.ShapeDtypeStruct((b, s, nh * HEAD_DIM), BF16),
        compiler_params=pltpu.CompilerParams(
            dimension_semantics=("arbitrary", "arbitrary", "arbitrary"), vmem_limit_bytes=VMEM_LIMIT),
        name="attention",
    )(first, stab, perm.reshape(-1), *([qt] * hps), *([k] * hps), *([vt] * hps), jnp.eye(tq, dtype=BF16))


def _attn_out_kernel(o_ref, x_ref, g_ref, wo_ref, wrh_ref, wrl_ref, ltri_ref, x3_ref, ri_ref, rw_ref, cnt_ref,
                     carry_ref, *, tm, sub):
    i = pl.program_id(0)
    g = g_ref[...]

    @pl.when(i == 0)
    def _():
        carry_ref[...] = jnp.zeros_like(carry_ref)

    carry = carry_ref[...]
    lane = lax.broadcasted_iota(jnp.int32, (sub, LANES), 1)
    for r0 in range(0, tm, sub):
        rows = pl.ds(r0, sub)
        y = jnp.dot(o_ref[rows, :], wo_ref[0], preferred_element_type=F32)
        x3 = x_ref[rows, :] + _rms(y, g[0:1])
        x3_ref[rows, :] = x3
        h = _rms(x3, g[1:2])
        h_hi = h.astype(BF16)
        h_lo = (h - h_hi.astype(F32)).astype(BF16)
        logits = (jnp.dot(h_hi, wrh_ref[...], preferred_element_type=F32)
                  + jnp.dot(h_lo, wrh_ref[...], preferred_element_type=F32)
                  + jnp.dot(h_hi, wrl_ref[...], preferred_element_type=F32))
        logits = jnp.where(lane < N_EXPERTS, logits, -jnp.inf)
        m1 = jnp.max(logits, axis=-1, keepdims=True)
        e1 = jnp.min(jnp.where(logits == m1, lane, LANES), axis=-1, keepdims=True)
        rest = jnp.where(lane == e1, -jnp.inf, logits)
        m2 = jnp.max(rest, axis=-1, keepdims=True)
        e2 = jnp.min(jnp.where(rest == m2, lane, LANES), axis=-1, keepdims=True)
        t = jnp.exp(m2 - m1)
        w1 = 1.0 / (1.0 + t)
        w2 = t / (1.0 + t)

        hot1 = lane == e1
        hot2 = lane == e2
        cnt = jnp.where(hot1 | hot2, 1.0, 0.0)
        before = jnp.dot(ltri_ref[...], cnt.astype(BF16), preferred_element_type=F32) + carry
        r1 = jnp.sum(jnp.where(hot1, before, 0.0), axis=-1, keepdims=True)
        r2 = jnp.sum(jnp.where(hot2, before, 0.0), axis=-1, keepdims=True)
        carry = carry + jnp.sum(cnt, axis=0, keepdims=True)
        ri = jnp.where(lane == 0, e1, jnp.where(lane == 1, e2, jnp.where(
            lane == 2, r1.astype(jnp.int32), jnp.where(lane == 3, r2.astype(jnp.int32), 0))))
        ri_ref[:, rows] = ri.T[:META_LANES, :]
        rw_ref[rows, :] = jnp.where(lane == 0, w1, jnp.where(lane == 1, w2, 0.0))[:, :META_LANES]

    carry_ref[...] = carry
    cnt_ref[...] = jnp.broadcast_to(carry, cnt_ref.shape).astype(jnp.int32)


def _attn_out(o, x, g2, w_out, w_router_pad, *, tm, tiles_per_batch):
    n, d = x.shape
    sub = min(CHAIN_ROWS, tm)
    ltri = jnp.tril(jnp.ones((sub, sub), BF16), k=-1)
    wr_hi = w_router_pad.astype(BF16)
    wr_lo = (w_router_pad - wr_hi.astype(F32)).astype(BF16)
    kern = functools.partial(_attn_out_kernel, tm=tm, sub=sub)
    row_spec = pl.BlockSpec((tm, d), lambda i: (i, 0))
    meta_spec = pl.BlockSpec((tm, META_LANES), lambda i: (i, 0))
    return pl.pallas_call(
        kern,
        grid=(n // tm,),
        in_specs=[row_spec, row_spec, _resident((2, d)),
                  pl.BlockSpec((1, d, d), lambda i: (i // tiles_per_batch, 0, 0)),
                  _resident(wr_hi.shape), _resident(wr_lo.shape), _resident((sub, sub))],
        out_specs=[row_spec, pl.BlockSpec((META_LANES, tm), lambda i: (0, i)), meta_spec,
                   pl.BlockSpec((8, LANES), lambda i: (0, 0))],
        out_shape=[jax.ShapeDtypeStruct((n, d), F32),
                   jax.ShapeDtypeStruct((META_LANES, n), jnp.int32),
                   jax.ShapeDtypeStruct((n, META_LANES), F32),
                   jax.ShapeDtypeStruct((8, LANES), jnp.int32)],
        scratch_shapes=[pltpu.VMEM((1, LANES), F32)],
        compiler_params=pltpu.CompilerParams(
            dimension_semantics=("arbitrary",), vmem_limit_bytes=VMEM_LIMIT),
        name="attn_out",
    )(o, x, g2, w_out, wr_hi, wr_lo, ltri)


def _row_copy(src, dst, sem):
    return pltpu.make_async_copy(src, dst, sem)


def _dispatch_kernel(fill_ref, pos_ref, x_ref, g_ref, xs_ref, h_buf, z_buf, sem, zsem, *, tb):
    i = pl.program_id(0)
    last = pl.num_programs(0) - 1
    slot = lax.rem(i, 2)
    h_buf[slot] = _rms(x_ref[...], g_ref[...])

    def issue(t, _):
        for which in range(2):
            p = pos_ref[0, which, t]
            _row_copy(h_buf.at[slot, pl.ds(t, 1), :], xs_ref.at[pl.ds(p, 1), :], sem.at[slot]).start(priority=which)
        return 0

    lax.fori_loop(0, tb, issue, 0, unroll=ISSUE_UNROLL)

    def drain(s):
        for _ in range(2 * tb):
            _row_copy(h_buf.at[s, pl.ds(0, 1), :], xs_ref.at[pl.ds(0, 1), :], sem.at[s]).wait()

    @pl.when(i == 0)
    def _():
        z_buf[...] = jnp.zeros_like(z_buf)
        for e in range(N_EXPERTS):
            lo = fill_ref[0, e]
            hi = fill_ref[1, e]

            def fill(r, _):
                _row_copy(z_buf, xs_ref.at[pl.ds(r, 1), :], zsem).start()
                return 0

            lax.fori_loop(lo, hi, fill, 0)

            def fill_done(r, _):
                _row_copy(z_buf, xs_ref.at[pl.ds(0, 1), :], zsem).wait()
                return 0

            lax.fori_loop(lo, hi, fill_done, 0)

    @pl.when(i > 0)
    def _():
        drain(1 - slot)

    @pl.when(i == last)
    def _():
        drain(slot)


def _dispatch(x3, g, pos, fill, n_rows, *, tb):
    n, d = x3.shape
    kern = functools.partial(_dispatch_kernel, tb=tb)
    return pl.pallas_call(
        kern,
        grid_spec=pltpu.PrefetchScalarGridSpec(
            num_scalar_prefetch=1,
            grid=(n // tb,),
            in_specs=[
                pl.BlockSpec((1, 2, tb), lambda i, f: (i, 0, 0), memory_space=pltpu.SMEM),
                pl.BlockSpec((tb, d), lambda i, f: (i, 0)),
                pl.BlockSpec((1, d), lambda i, f: (0, 0)),
            ],
            out_specs=pl.BlockSpec(memory_space=pl.ANY),
            scratch_shapes=[pltpu.VMEM((2, tb, d), F32), pltpu.VMEM((1, d), F32),
                            pltpu.SemaphoreType.DMA((2,)), pltpu.SemaphoreType.DMA(())],
        ),
        out_shape=jax.ShapeDtypeStruct((n_rows, d), F32),
        compiler_params=pltpu.CompilerParams(
            dimension_semantics=("arbitrary",), vmem_limit_bytes=VMEM_LIMIT),
        name="dispatch",
    )(fill, pos, x3, g)


def _experts_kernel(te_ref, nt_ref, xs_ref, wg_ref, wu_ref, wd_ref, ys_ref, xb_ref, acc_ref, *, nj):
    i = pl.program_id(0)
    j = pl.program_id(1)
    live = i < nt_ref[0]
    tm = xb_ref.shape[0]
    sub = min(CHAIN_ROWS, tm)

    def partial_out(xb):
        gate = jnp.dot(xb, wg_ref[0], preferred_element_type=F32)
        up = jnp.dot(xb, wu_ref[0], preferred_element_type=F32)
        act = (gate * jax.nn.sigmoid(gate) * up).astype(BF16)
        return jnp.dot(act, wd_ref[0], preferred_element_type=F32)

    @pl.when(live & (j == 0))
    def _():
        for r0 in range(0, tm, sub):
            xb = xs_ref[r0:r0 + sub, :].astype(BF16)
            if nj > 1:
                xb_ref[r0:r0 + sub, :] = xb
                acc_ref[r0:r0 + sub, :] = partial_out(xb)
            else:
                ys_ref[r0:r0 + sub, :] = partial_out(xb)

    if nj > 2:
        @pl.when(live & (j > 0) & (j < nj - 1))
        def _():
            for r0 in range(0, tm, sub):
                acc_ref[r0:r0 + sub, :] += partial_out(xb_ref[r0:r0 + sub, :])

    if nj > 1:
        @pl.when(live & (j == nj - 1))
        def _():
            for r0 in range(0, tm, sub):
                ys_ref[r0:r0 + sub, :] = acc_ref[r0:r0 + sub, :] + partial_out(xb_ref[r0:r0 + sub, :])

    @pl.when(jnp.logical_not(live) & (j == nj - 1))
    def _():
        ys_ref[...] = jnp.zeros_like(ys_ref)


def _experts(xs, tile_expert, n_tiles, w_gate, w_up, w_down, *, tm, tf):
    n_rows, d = xs.shape
    dff = w_gate.shape[-1]
    max_tiles = n_rows // tm

    def row_map(i, j, te, nt):
        return (jnp.minimum(i, nt[0] - 1), 0)

    def wcol_map(i, j, te, nt):
        return (te[i], 0, j)

    def wrow_map(i, j, te, nt):
        return (te[i], j, 0)

    return pl.pallas_call(
        functools.partial(_experts_kernel, nj=dff // tf),
        grid_spec=pltpu.PrefetchScalarGridSpec(
            num_scalar_prefetch=2,
            grid=(max_tiles, dff // tf),
            in_specs=[
                pl.BlockSpec((tm, d), row_map),
                pl.BlockSpec((1, d, tf), wcol_map),
                pl.BlockSpec((1, d, tf), wcol_map),
                pl.BlockSpec((1, tf, d), wrow_map),
            ],
            out_specs=pl.BlockSpec((tm, d), lambda i, j, te, nt: (i, 0)),
            scratch_shapes=[pltpu.VMEM((tm, d), BF16), pltpu.VMEM((tm, d), F32)],
        ),
        out_shape=jax.ShapeDtypeStruct((n_rows, d), F32),
        compiler_params=pltpu.CompilerParams(
            dimension_semantics=("arbitrary", "arbitrary"), vmem_limit_bytes=VMEM_LIMIT),
        name="experts",
    )(tile_expert, n_tiles, xs, w_gate, w_up, w_down)


def _combine_kernel(pos_ref, posn_ref, ys_ref, x_ref, rw_ref, g_ref, o_ref, a_buf, b_buf, sem, *, tb):
    i = pl.program_id(0)
    last = pl.num_programs(0) - 1
    slot = lax.rem(i, 2)

    def gather(idx_ref, s):
        def issue(t, _):
            _row_copy(ys_ref.at[pl.ds(idx_ref[0, 0, t], 1), :], a_buf.at[s, pl.ds(t, 1), :], sem.at[s]).start()
            _row_copy(ys_ref.at[pl.ds(idx_ref[0, 1, t], 1), :], b_buf.at[s, pl.ds(t, 1), :],
                      sem.at[s]).start(priority=1)
            return 0

        lax.fori_loop(0, tb, issue, 0, unroll=ISSUE_UNROLL)

    @pl.when(i == 0)
    def _():
        gather(pos_ref, slot)

    @pl.when(i < last)
    def _():
        gather(posn_ref, 1 - slot)

    for _ in range(2 * tb):
        _row_copy(ys_ref.at[pl.ds(0, 1), :], a_buf.at[slot, pl.ds(0, 1), :], sem.at[slot]).wait()
    rw = rw_ref[...]
    y = rw[:, 0:1] * a_buf[slot] + rw[:, 1:2] * b_buf[slot]
    o_ref[...] = x_ref[...] + _rms(y, g_ref[...])


def _combine(ys, x3, rw, g, pos, *, tb):
    n, d = x3.shape
    n_tiles = n // tb
    kern = functools.partial(_combine_kernel, tb=tb)
    return pl.pallas_call(
        kern,
        grid=(n_tiles,),
        in_specs=[
            pl.BlockSpec((1, 2, tb), lambda i: (i, 0, 0), memory_space=pltpu.SMEM),
            pl.BlockSpec((1, 2, tb), lambda i: (jnp.minimum(i + 1, n_tiles - 1), 0, 0), memory_space=pltpu.SMEM),
            pl.BlockSpec(memory_space=pl.ANY),
            pl.BlockSpec((tb, d), lambda i: (i, 0)),
            pl.BlockSpec((tb, META_LANES), lambda i: (i, 0)),
            pl.BlockSpec((1, d), lambda i: (0, 0)),
        ],
        out_specs=pl.BlockSpec((tb, d), lambda i: (i, 0)),
        out_shape=jax.ShapeDtypeStruct((n, d), F32),
        scratch_shapes=[pltpu.VMEM((2, tb, d), F32), pltpu.VMEM((2, tb, d), F32),
                        pltpu.SemaphoreType.DMA((2,))],
        compiler_params=pltpu.CompilerParams(
            dimension_semantics=("arbitrary",), vmem_limit_bytes=VMEM_LIMIT),
        name="combine",
    )(pos, pos, ys, x3, rw, g)


def _pick_tile(n, pref):
    t = min(pref, n)
    while n % t:
        t //= 2
    return t


def kernel(x, norm_g, pool_w, pool_scale, attn_w_in, attn_b_f, attn_w_out, ffn_w_gate, ffn_w_up, ffn_w_down,
           moe_w_router, moe_w_gate, moe_w_up, moe_w_down):
    b, s, d = x.shape
    n = b * s
    nh = d // HEAD_DIM
    tq = _pick_tile(s, 512)
    tm = _pick_tile(s, 512)
    tm_qkv = _pick_tile(s, 1024)

    x = _layer0(x, norm_g[0], pool_w[0].astype(BF16), pool_scale[0],
                ffn_w_gate[0].astype(BF16), ffn_w_up[0].astype(BF16), ffn_w_down[0].astype(BF16), tm=tm)

    w_in = attn_w_in[0]
    w_f = jnp.zeros((d, LANES), F32).at[:, :nh].set(w_in[:, 3 * d:]).astype(BF16)
    b_f = jnp.zeros((1, LANES), F32).at[0, :nh].set(attn_b_f[0])
    w_qvt = jnp.concatenate([w_in[:, :d], w_in[:, 2 * d:3 * d]], axis=1).T.astype(BF16)
    qt, k, vt, c2, qn, kn = _qkv(x, norm_g[1, 0:1], w_in[:, d:2 * d].astype(BF16), w_qvt, w_f, b_f, tm=tm_qkv)
    first, stab, stab_ok, perm = _first_live_block(c2, qn, kn, tq=tq, heads_per_step=ATTN_HEADS_PER_STEP)
    o = lax.cond(stab_ok,
                 functools.partial(_attention, tq=tq, fixed_stabiliser=True),
                 functools.partial(_attention, tq=tq, fixed_stabiliser=False),
                 qt, k, vt, first, stab, perm)
    w_out = attn_w_out[0].astype(BF16).reshape(nh, HEAD_DIM, d)[perm].reshape(b, d, d)

    w_router = jnp.zeros((d, LANES), F32).at[:, :N_EXPERTS].set(moe_w_router[0])
    x3, ri, rw, counts = _attn_out(o.reshape(n, d), x.reshape(n, d), norm_g[1, 1:3], w_out, w_router,
                                   tm=tm, tiles_per_batch=s // tm)

    tme = _pick_tile(n, 512)
    counts = counts[0, :N_EXPERTS]
    padded = (counts + tme - 1) // tme * tme
    ends = jnp.cumsum(padded)
    starts = ends - padded
    n_rows = 2 * n + N_EXPERTS * tme
    pos = jnp.stack([starts[ri[0]] + ri[2], starts[ri[1]] + ri[3]], axis=0)
    tb = _pick_tile(n, 256)
    pos = pos.reshape(2, n // tb, tb).transpose(1, 0, 2)
    fill_hi = ends.at[N_EXPERTS - 1].set(n_rows)
    fill = jnp.stack([starts + counts, fill_hi], axis=0).astype(jnp.int32)
    tile_start = jnp.arange(n_rows // tme, dtype=jnp.int32) * tme
    tile_expert = jnp.minimum(jnp.sum(tile_start[:, None] >= ends[None, :], axis=1), N_EXPERTS - 1).astype(jnp.int32)
    n_tiles = (ends[-1:] // tme).astype(jnp.int32)

    xs = _dispatch(x3, norm_g[1, 2:3], pos, fill, n_rows, tb=tb)
    dffe = moe_w_gate.shape[-1]
    tf = dffe // 2 if (dffe // 2) % LANES == 0 else dffe
    ys = _experts(xs, tile_expert, n_tiles, moe_w_gate[0].astype(BF16), moe_w_up[0].astype(BF16),
                  moe_w_down[0].astype(BF16), tm=tme, tf=tf)
    out = _combine(ys, x3, rw, norm_g[1, 3:4], pos, tb=tb)
    return out.reshape(b, s, d)
```
